```python
import jax, jax.numpy as jnp
from jax import lax
import numpy as np

D_MODEL = 2048
BATCH = 4
SEQ = 2048
DEPTH = 4
DEC_BATCH = 8
DEC_SEQ = 4
PAST_LEN = 16384
PAGE_SIZE = 128

HEAD_DIM = 128
MIX_W = D_MODEL
MEM_W = MIX_W // 4
MAIN_W = MIX_W - MEM_W
SB_HEADS = MAIN_W // HEAD_DIM
MEM_HEADS = MEM_W // HEAD_DIM
N_MEM = 256
POOL_WINDOWS = (2, 4, 8, 16)
POOL_GROUPS = len(POOL_WINDOWS)
POOL_CG = MAIN_W // POOL_GROUPS
POOL_STATE = max(POOL_WINDOWS) - 1
N_A_LAYERS = DEPTH // 2
N_B_LAYERS = DEPTH - N_A_LAYERS
N_GROUPS = 4
EXP_PER_GROUP = 4
N_EXPERTS = N_GROUPS * EXP_PER_GROUP
TOP_K_IN_GROUP = 2
EXPERT_FF = D_MODEL // 4
Q_BLOCK = 128
LN_EPS = 1e-5
ALPHA = (2 * DEPTH) ** 0.25
BETA_INIT = (8 * DEPTH) ** -0.25
SB_BIAS_INIT = -6.0

kernel_name = 'yoco_pool_stickbreak_hmoe_step'


def _layer_norm(x, g, b):
    xf = x.astype(jnp.float32)
    mu = jnp.mean(xf, axis=-1, keepdims=True)
    var = jnp.mean(jnp.square(xf - mu), axis=-1, keepdims=True)
    return ((xf - mu) * lax.rsqrt(var + LN_EPS) * g + b).astype(x.dtype)


def _pool_mix(u, prev, pos0, w_grp, scale):
    B, T, C = u.shape
    P = prev.shape[1]
    cat = jnp.concatenate([prev.astype(u.dtype), u], axis=1)
    cs = jnp.cumsum(cat.astype(jnp.float32), axis=1)
    cs = jnp.concatenate([jnp.zeros((B, 1, C), jnp.float32), cs], axis=1)
    pos = pos0 + jnp.arange(T)
    diffs = []
    for g, w in enumerate(POOL_WINDOWS):
        sl = slice(g * POOL_CG, (g + 1) * POOL_CG)
        win_sum = cs[:, P + 1:P + 1 + T, sl] - cs[:, P + 1 - w:P + 1 - w + T, sl]
        cnt = jnp.minimum(w, pos + 1).astype(jnp.float32)[None, :, None]
        diffs.append(win_sum / cnt - u[..., sl].astype(jnp.float32))
    d = jnp.stack(diffs, axis=2).astype(u.dtype)
    y = jnp.einsum('btgc,gce->btge', d, w_grp).reshape(B, T, C) * scale
    return y, cat[:, -P:, :]


def _stick_breaking(q, k, v, bias, q_pos, k_pos):
    B, T, H, D = q.shape
    blk = min(Q_BLOCK, T)
    nb = T // blk
    qb = q.reshape(B, nb, blk, H, D).transpose(1, 0, 2, 3, 4)
    pb = q_pos.reshape(nb, blk)
    inv_sqrt = 1.0 / np.sqrt(D).astype(np.float32)
    bias_f = bias.astype(jnp.float32)[None, :, None, None]

    def one_block(args):
        qi, pi = args
        z = jnp.einsum('bqhd,bshd->bhqs', qi, k).astype(jnp.float32) * inv_sqrt + bias_f
        mask = (k_pos[None, :] < pi[:, None])[None, None]
        log_1m = jnp.where(mask, jax.nn.log_sigmoid(-z), 0.0)
        after = lax.cumsum(log_1m, axis=3, reverse=True) - log_1m
        a = jnp.where(mask, jnp.exp(jax.nn.log_sigmoid(z) + after), 0.0)
        return jnp.einsum('bhqs,bshd->bqhd', a.astype(v.dtype), v)

    o = lax.map(one_block, (qb, pb))
    return o.transpose(1, 0, 2, 3, 4).reshape(B, T, H, D)


def _mem_attend(qm, mk, mv):
    s = jnp.einsum('bthd,bmhd->bhtm', qm, mk).astype(jnp.float32) * (HEAD_DIM ** -0.5)
    p = jax.nn.softmax(s, axis=-1)
    return jnp.einsum('bhtm,bmhd->bthd', p.astype(mv.dtype), mv)


def _hier_moe(x, wrg, brg, wre, bre, wg, wu, wd):
    lg = jnp.einsum('nd,dg->ng', x, wrg).astype(jnp.float32) + brg
    pg = jax.nn.softmax(lg, axis=-1)
    g_star = jnp.argmax(lg, axis=-1)
    oh_g = jax.nn.one_hot(g_star, N_GROUPS, dtype=jnp.float32)
    p_sel = jnp.sum(pg * oh_g, axis=-1)
    le = jnp.einsum('nd,dge->nge', x, wre).astype(jnp.float32) + bre
    le_sel = jnp.take_along_axis(le, g_star[:, None, None], axis=1)[:, 0]
    top_v, top_i = lax.top_k(le_sel, TOP_K_IN_GROUP)
    w2 = jax.nn.softmax(top_v, axis=-1)
    gate_e = jnp.sum(w2[..., None] * jax.nn.one_hot(top_i, EXP_PER_GROUP, dtype=jnp.float32), axis=1)
    gate = (p_sel[:, None, None] * oh_g[:, :, None] * gate_e[:, None, :]).reshape(-1, N_EXPERTS)
    h = jax.nn.silu(jnp.einsum('nd,xdf->nxf', x, wg)) * jnp.einsum('nd,xdf->nxf', x, wu)
    return jnp.einsum('nxf,xfd->nd', h * gate[..., None].astype(h.dtype), wd)


def _trunk(x, pool_prev, mem_k, mem_v, past_k, past_v, w_in, w_out, w_pool_grp, pool_scale,
           ln1_g, ln1_b, ln2_g, ln2_b, w_route_grp, b_route_grp, w_route_exp, b_route_exp,
           w_gate, w_up, w_down, w_sb_k, w_sb_v, sb_bias):
    B, T, _ = x.shape
    pos0 = past_k.shape[1]
    q_pos = pos0 + jnp.arange(T)
    k_pos = jnp.arange(pos0 + T)
    new_pool = []
    k_all = v_all = new_k = new_v = None
    for l in range(DEPTH):
        proj = jnp.einsum('btd,df->btf', x, w_in[l])
        main, qm = proj[..., :MAIN_W], proj[..., MAIN_W:]
        if l < N_A_LAYERS:
            y_main, st = _pool_mix(main, pool_prev[l], pos0, w_pool_grp[l], pool_scale[l])
            new_pool.append(st)
        else:
            q = main.reshape(B, T, SB_HEADS, HEAD_DIM)
            y_main = _stick_breaking(q, k_all, v_all, sb_bias[l - N_A_LAYERS], q_pos, k_pos).reshape(B, T, MAIN_W)
        y_mem = _mem_attend(qm.reshape(B, T, MEM_HEADS, HEAD_DIM), mem_k[l], mem_v[l]).reshape(B, T, MEM_W)
        h = jnp.einsum('btf,fd->btd', jnp.concatenate([y_main, y_mem], axis=-1), w_out[l])
        x = _layer_norm(ALPHA * x + h, ln1_g[l], ln1_b[l])
        m = _hier_moe(x.reshape(B * T, D_MODEL), w_route_grp[l], b_route_grp[l], w_route_exp[l],
                      b_route_exp[l], w_gate[l], w_up[l], w_down[l]).reshape(B, T, D_MODEL)
        x = _layer_norm(ALPHA * x + m, ln2_g[l], ln2_b[l])
        if l == N_A_LAYERS - 1:
            new_k = jnp.einsum('btd,df->btf', x, w_sb_k).reshape(B, T, SB_HEADS, HEAD_DIM)
            new_v = jnp.einsum('btd,df->btf', x, w_sb_v).reshape(B, T, SB_HEADS, HEAD_DIM)
            k_all = jnp.concatenate([past_k.astype(x.dtype), new_k], axis=1)
            v_all = jnp.concatenate([past_v.astype(x.dtype), new_v], axis=1)
    return x, jnp.stack(new_pool, axis=0), new_k, new_v


def setup_inputs(seed: int = 0) -> dict:
    key = jax.random.key(seed)
    ks = jax.random.split(key, 32)
    f32 = jnp.float32

    def nrm(k, shape, s=1.0):
        return jax.random.normal(k, shape, f32) * s

    n_pages = PAST_LEN // PAGE_SIZE
    n_used = DEC_BATCH * n_pages
    n_phys = n_used + max(1, n_used // 4)
    dsc = D_MODEL ** -0.5
    return {
        'x_prompt': nrm(ks[0], (BATCH, SEQ, D_MODEL)),
        'x_sample': nrm(ks[1], (DEC_BATCH, DEC_SEQ, D_MODEL)),
        'state_pool': nrm(ks[2], (N_A_LAYERS, DEC_BATCH, POOL_STATE, MAIN_W)),
        'cache_sb_k': nrm(ks[3], (n_phys, PAGE_SIZE, SB_HEADS, HEAD_DIM)),
        'cache_sb_v': nrm(ks[4], (n_phys, PAGE_SIZE, SB_HEADS, HEAD_DIM), BETA_INIT),
        'cache_mem_k': nrm(ks[5], (DEPTH, DEC_BATCH, N_MEM, MEM_HEADS, HEAD_DIM)),
        'cache_mem_v': nrm(ks[6], (DEPTH, DEC_BATCH, N_MEM, MEM_HEADS, HEAD_DIM), BETA_INIT),
        'page_table': jax.random.permutation(ks[7], n_phys)[:n_used].reshape(DEC_BATCH, n_pages).astype(jnp.int32),
        'mem_prompt': nrm(ks[8], (BATCH, N_MEM, D_MODEL)),
        'w_in': nrm(ks[9], (DEPTH, D_MODEL, MIX_W), dsc),
        'w_out': nrm(ks[10], (DEPTH, MIX_W, D_MODEL), MIX_W ** -0.5 * BETA_INIT),
        'w_pool_grp': nrm(ks[11], (N_A_LAYERS, POOL_GROUPS, POOL_CG, POOL_CG), POOL_CG ** -0.5),
        'pool_scale': 1.0 + nrm(ks[12], (N_A_LAYERS, MAIN_W), 0.02),
        'w_mem_k': nrm(ks[13], (DEPTH, D_MODEL, MEM_W), dsc),
        'w_mem_v': nrm(ks[14], (DEPTH, D_MODEL, MEM_W), dsc * BETA_INIT),
        'ln1_g': 1.0 + nrm(ks[15], (DEPTH, D_MODEL), 0.02),
        'ln1_b': nrm(ks[16], (DEPTH, D_MODEL), 0.02),
        'ln2_g': 1.0 + nrm(ks[17], (DEPTH, D_MODEL), 0.02),
        'ln2_b': nrm(ks[18], (DEPTH, D_MODEL), 0.02),
        'w_route_grp': nrm(ks[19], (DEPTH, D_MODEL, N_GROUPS), dsc),
        'b_route_grp': nrm(ks[20], (DEPTH, N_GROUPS), 0.01),
        'w_route_exp': nrm(ks[21], (DEPTH, D_MODEL, N_GROUPS, EXP_PER_GROUP), dsc),
        'b_route_exp': nrm(ks[22], (DEPTH, N_GROUPS, EXP_PER_GROUP), 0.01),
        'w_gate': nrm(ks[23], (DEPTH, N_EXPERTS, D_MODEL, EXPERT_FF), dsc),
        'w_up': nrm(ks[24], (DEPTH, N_EXPERTS, D_MODEL, EXPERT_FF), dsc),
        'w_down': nrm(ks[25], (DEPTH, N_EXPERTS, EXPERT_FF, D_MODEL), EXPERT_FF ** -0.5 * BETA_INIT),
        'w_sb_k': nrm(ks[26], (D_MODEL, MAIN_W), dsc),
        'w_sb_v': nrm(ks[27], (D_MODEL, MAIN_W), dsc * BETA_INIT),
        'sb_bias': SB_BIAS_INIT + nrm(ks[28], (N_B_LAYERS, SB_HEADS), 0.1),
    }


def reference(x_prompt, x_sample, state_pool, cache_sb_k, cache_sb_v, cache_mem_k, cache_mem_v,
              page_table, mem_prompt, w_in, w_out, w_pool_grp, pool_scale, w_mem_k, w_mem_v,
              ln1_g, ln1_b, ln2_g, ln2_b, w_route_grp, b_route_grp, w_route_exp, b_route_exp,
              w_gate, w_up, w_down, w_sb_k, w_sb_v, sb_bias):
    bp = x_prompt.shape[0]
    mem_k_p = jnp.einsum('bmd,ldf->lbmf', mem_prompt, w_mem_k).reshape(DEPTH, bp, N_MEM, MEM_HEADS, HEAD_DIM)
    mem_v_p = jnp.einsum('bmd,ldf->lbmf', mem_prompt, w_mem_v).reshape(DEPTH, bp, N_MEM, MEM_HEADS, HEAD_DIM)
    pool0 = jnp.zeros((N_A_LAYERS, bp, POOL_STATE, MAIN_W), x_prompt.dtype)
    past0 = jnp.zeros((bp, 0, SB_HEADS, HEAD_DIM), x_prompt.dtype)
    y_p, pool_p, k_p, v_p = _trunk(
        x_prompt, pool0, mem_k_p, mem_v_p, past0, past0, w_in, w_out, w_pool_grp, pool_scale,
        ln1_g, ln1_b, ln2_g, ln2_b, w_route_grp, b_route_grp, w_route_exp, b_route_exp,
        w_gate, w_up, w_down, w_sb_k, w_sb_v, sb_bias)
    n_dec, n_pages = page_table.shape
    past_len = n_pages * cache_sb_k.shape[1]
    past_k = cache_sb_k[page_table].reshape(n_dec, past_len, SB_HEADS, HEAD_DIM)
    past_v = cache_sb_v[page_table].reshape(n_dec, past_len, SB_HEADS, HEAD_DIM)
    y_s, pool_s, k_s, v_s = _trunk(
        x_sample, state_pool, cache_mem_k, cache_mem_v, past_k, past_v, w_in, w_out, w_pool_grp,
        pool_scale, ln1_g, ln1_b, ln2_g, ln2_b, w_route_grp, b_route_grp, w_route_exp, b_route_exp,
        w_gate, w_up, w_down, w_sb_k, w_sb_v, sb_bias)
    return (y_p, y_s, pool_p, pool_s, k_p, v_p, k_s, v_s, mem_k_p, mem_v_p)
```

```python
import functools

import jax
import jax.numpy as jnp
import numpy as np
from jax import lax
from jax.experimental import pallas as pl
from jax.experimental.pallas import tpu as pltpu

F32 = jnp.float32
BF = jnp.bfloat16
I32 = jnp.int32

HEAD_DIM = 128
POOL_WINDOWS = (2, 4, 8, 16)
POOL_STATE = max(POOL_WINDOWS) - 1
N_GROUPS = 4
EXP_PER_GROUP = 4
N_EXPERTS = N_GROUPS * EXP_PER_GROUP
LN_EPS = 1e-5
ROUTE_ROWS = 32
VMEM_LIMIT = 56 * 2**20

_NT = (((1,), (1,)), ((), ()))


def _params(sem):
    return pltpu.CompilerParams(dimension_semantics=sem, vmem_limit_bytes=VMEM_LIMIT)


def _dot(a, b):
    return jnp.dot(a, b, preferred_element_type=F32)


def _layer_norm(v, g, b):
    mu = jnp.mean(v, axis=-1, keepdims=True)
    c = v - mu
    var = jnp.mean(c * c, axis=-1, keepdims=True)
    return c * lax.rsqrt(var + LN_EPS) * g + b


def _softplus(z):
    return jnp.maximum(z, 0.0) + jnp.log1p(jnp.exp(-jnp.abs(z)))


def _split_bf16(x):
    hi = x.astype(BF)
    lo = (x - hi.astype(F32)).astype(BF)
    return hi, lo


def _mm_kernel(x_ref, w_ref, *o_refs, offsets):
    acc = _dot(x_ref[...].astype(BF), w_ref[...])
    for o_ref, off in zip(o_refs, offsets):
        o_ref[...] = acc[:, off:off + o_ref.shape[-1]].astype(o_ref.dtype)


def _matmul(x, w, outs, tm, name):
    m, k = x.shape
    n = w.shape[1]
    return pl.pallas_call(
        functools.partial(_mm_kernel, offsets=tuple(o for o, _, _ in outs)),
        grid=(m // tm,),
        in_specs=[pl.BlockSpec((tm, k), lambda i: (i, 0)),
                  pl.BlockSpec((k, n), lambda i: (0, 0))],
        out_specs=[pl.BlockSpec((tm, c), lambda i: (i, 0)) for _, c, _ in outs],
        out_shape=[jax.ShapeDtypeStruct((m, c), dt) for _, c, dt in outs],
        compiler_params=_params(("parallel",)),
        name=name,
    )(x, w)


def _memproj_kernel(x_ref, w_ref, o_ref):
    o_ref[0] = _dot(x_ref[...], w_ref[...])


def _mem_project(mem_bf, w_cat, width):
    m, d = mem_bf.shape
    nj = w_cat.shape[1] // width
    return pl.pallas_call(
        _memproj_kernel,
        grid=(nj,),
        in_specs=[pl.BlockSpec((m, d), lambda j: (0, 0)),
                  pl.BlockSpec((d, width), lambda j: (0, j))],
        out_specs=pl.BlockSpec((1, m, width), lambda j: (j, 0, 0)),
        out_shape=jax.ShapeDtypeStruct((nj, m, width), F32),
        compiler_params=_params(("parallel",)),
        name="mem_project",
    )(mem_bf, w_cat)


def _pool_kernel(u_ref, prev_ref, w_ref, sc_ref, y_ref, buf, *, tt, nt, pos0):
    t = pl.program_id(1)
    p = POOL_STATE
    cg = w_ref.shape[-1]

    @pl.when(t == 0)
    def _():
        buf[1:1 + p, :] = prev_ref[0]

    u = u_ref[0]
    buf[p + 1:p + 1 + tt, :] = u
    pos = lax.broadcasted_iota(I32, (tt, 1), 0) + (t * tt + pos0)
    for g, w in enumerate(POOL_WINDOWS):
        c0 = g * cg
        ug = u[:, c0:c0 + cg]
        s = ug
        for k in range(1, w):
            s = s + buf[p + 1 - k:p + 1 - k + tt, c0:c0 + cg]
        cnt = jnp.minimum(w, pos + 1).astype(F32)
        d = (s / cnt - ug).astype(BF)
        yg = _dot(d, w_ref[g]) * sc_ref[:, c0:c0 + cg]
        y_ref[0, :, c0:c0 + cg] = yg.astype(y_ref.dtype)
    if nt > 1:
        buf[1:1 + p, :] = buf[tt + 1:tt + 1 + p, :]


def _pool_mix(proj, prev, w_grp_bf, scale, main_w, tt, pos0):
    b, t, _ = proj.shape
    nt = t // tt
    cg = main_w // len(POOL_WINDOWS)
    return pl.pallas_call(
        functools.partial(_pool_kernel, tt=tt, nt=nt, pos0=pos0),
        grid=(b, nt),
        in_specs=[pl.BlockSpec((1, tt, main_w), lambda i, j: (i, j, 0)),
                  pl.BlockSpec((1, POOL_STATE, main_w), lambda i, j: (i, 0, 0)),
                  pl.BlockSpec((len(POOL_WINDOWS), cg, cg), lambda i, j: (0, 0, 0)),
                  pl.BlockSpec((1, main_w), lambda i, j: (0, 0))],
        out_specs=pl.BlockSpec((1, tt, main_w), lambda i, j: (i, j, 0)),
        out_shape=jax.ShapeDtypeStruct((b, t, main_w), BF),
        scratch_shapes=[pltpu.VMEM((POOL_STATE + 1 + tt, main_w), F32)],
        compiler_params=_params(("parallel", "arbitrary")),
        name="pool_mix",
    )(proj, prev, w_grp_bf, scale.reshape(1, main_w))


def _pool_dec_kernel(u_ref, prev_ref, w_ref, sc_ref, y_ref, *, pos0):
    p = POOL_STATE
    nt = u_ref.shape[0]
    cg = w_ref.shape[-1]
    cat = [prev_ref[k] for k in range(p)] + [u_ref[i] for i in range(nt)]
    for g, w in enumerate(POOL_WINDOWS):
        c0 = g * cg
        ds = []
        for i in range(nt):
            s = cat[p + i][:, c0:c0 + cg]
            for k in range(1, w):
                s = s + cat[p + i - k][:, c0:c0 + cg]
            cnt = float(min(w, pos0 + i + 1))
            ds.append(s / cnt - cat[p + i][:, c0:c0 + cg])
        d = jnp.concatenate(ds, axis=0).astype(BF)
        yg = _dot(d, w_ref[g]) * sc_ref[:, c0:c0 + cg]
        nb = ds[0].shape[0]
        for i in range(nt):
            y_ref[i, :, c0:c0 + cg] = yg[i * nb:(i + 1) * nb].astype(y_ref.dtype)


def _pool_mix_decode(u_tm, prev_tm, w_grp_bf, scale, pos0):
    nt, b, c = u_tm.shape
    return pl.pallas_call(
        functools.partial(_pool_dec_kernel, pos0=pos0),
        out_shape=jax.ShapeDtypeStruct((nt, b, c), BF),
        compiler_params=pltpu.CompilerParams(vmem_limit_bytes=VMEM_LIMIT),
        name="pool_mix_decode",
    )(u_tm, prev_tm, w_grp_bf, scale.reshape(1, c))


def _mem_kernel(q_ref, k_ref, v_ref, o_ref, *, cdt):
    nh = q_ref.shape[-1] // HEAD_DIM
    scale = HEAD_DIM ** -0.5
    for h in range(nh):
        sl = slice(h * HEAD_DIM, (h + 1) * HEAD_DIM)
        q = q_ref[0, :, sl].astype(cdt)
        k = k_ref[0, :, sl].astype(cdt)
        v = v_ref[0, :, sl].astype(cdt)
        s = lax.dot_general(q, k, _NT, preferred_element_type=F32) * scale
        e = jnp.exp(s - jnp.max(s, axis=-1, keepdims=True))
        den = jnp.sum(e, axis=-1, keepdims=True)
        o = _dot(e.astype(cdt), v) / den
        o_ref[0, :, sl] = o.astype(o_ref.dtype)


def _mem_attend(proj, mk, mv, mem_w, tt):
    b, t, mix_w = proj.shape
    n_mem = mk.shape[1]
    qblk = (mix_w - mem_w) // mem_w
    cdt = BF if tt % 16 == 0 else F32
    return pl.pallas_call(
        functools.partial(_mem_kernel, cdt=cdt),
        grid=(b, t // tt),
        in_specs=[pl.BlockSpec((1, tt, mem_w), lambda i, j: (i, j, qblk)),
                  pl.BlockSpec((1, n_mem, mem_w), lambda i, j: (i, 0, 0)),
                  pl.BlockSpec((1, n_mem, mem_w), lambda i, j: (i, 0, 0))],
        out_specs=pl.BlockSpec((1, tt, mem_w), lambda i, j: (i, j, 0)),
        out_shape=jax.ShapeDtypeStruct((b, t, mem_w), BF),
        compiler_params=_params(("parallel", "parallel")),
        name="mem_attend",
    )(proj, mk, mv)


def _sb_kernel(q_ref, k_ref, v_ref, bias_ref, o_ref, *, tq):
    qi = pl.program_id(2)
    inv_sqrt = 1.0 / np.sqrt(np.float32(HEAD_DIM))
    q = q_ref[0].astype(BF)
    bias = bias_ref[0]
    jj = lax.broadcasted_iota(I32, (tq, tq), 0)
    ss = lax.broadcasted_iota(I32, (tq, tq), 1)
    later = (jj > ss).astype(BF)
    rel = ss - jj

    def body(n, carry):
        tail, acc = carry
        kb = qi - n
        start = pl.multiple_of(kb * tq, tq)
        kblk = k_ref[0, pl.ds(start, tq), :]
        vblk = v_ref[0, pl.ds(start, tq), :]
        z = lax.dot_general(q, kblk, _NT, preferred_element_type=F32) * inv_sqrt + bias
        mask = rel < (qi - kb) * tq
        sp = _softplus(z)
        log_1m = jnp.where(mask, -sp, 0.0)
        hi, lo = _split_bf16(log_1m)
        after = _dot(hi, later) + _dot(lo, later) + tail
        a = jnp.where(mask, jnp.exp(z - sp + after), 0.0)
        acc = acc + _dot(a.astype(BF), vblk)
        tail = tail + jnp.sum(log_1m, axis=1, keepdims=True)
        return tail, acc

    init = (jnp.zeros((tq, 1), F32), jnp.zeros((tq, HEAD_DIM), F32))
    _, acc = lax.fori_loop(0, qi + 1, body, init)
    o_ref[0] = acc.astype(o_ref.dtype)


def _stick_breaking_prompt(proj, k_bf, v_bf, bias, n_heads, tq):
    b, t, _ = proj.shape
    main_w = n_heads * HEAD_DIM
    bias_b = jnp.broadcast_to(bias.astype(F32)[:, None, None], (n_heads, 1, tq))
    return pl.pallas_call(
        functools.partial(_sb_kernel, tq=tq),
        grid=(b, n_heads, t // tq),
        in_specs=[pl.BlockSpec((1, tq, HEAD_DIM), lambda i, h, j: (i, j, h)),
                  pl.BlockSpec((1, t, HEAD_DIM), lambda i, h, j: (i, 0, h)),
                  pl.BlockSpec((1, t, HEAD_DIM), lambda i, h, j: (i, 0, h)),
                  pl.BlockSpec((1, 1, tq), lambda i, h, j: (h, 0, 0))],
        out_specs=pl.BlockSpec((1, tq, HEAD_DIM), lambda i, h, j: (i, j, h)),
        out_shape=jax.ShapeDtypeStruct((b, t, main_w), BF),
        compiler_params=_params(("parallel", "parallel", "arbitrary")),
        name="stick_breaking_prompt",
    )(proj, k_bf, v_bf, bias_b)


def _sb_dec_kernel(pt_ref, qt_ref, bias_ref, kn_ref, vn_ref, *rest, pages, n_q, n_new):
    k_refs = rest[:pages]
    v_refs = rest[pages:2 * pages]
    o_ref, tail_ref, acc_ref = rest[2 * pages:]
    j = pl.program_id(1)
    inv_sqrt = 1.0 / np.sqrt(np.float32(HEAD_DIM))
    qt = qt_ref[0]
    bias = bias_ref[...]
    n_cols = acc_ref.shape[0]

    def block(kblk, vblk, mask):
        nk = kblk.shape[0]
        z = _dot(kblk.astype(BF), qt) * inv_sqrt + bias
        sp = _softplus(z)
        log_1m = -sp if mask is None else jnp.where(mask, -sp, 0.0)
        ss = lax.broadcasted_iota(I32, (nk, nk), 0)
        jj = lax.broadcasted_iota(I32, (nk, nk), 1)
        later = (jj > ss).astype(BF)
        hi, lo = _split_bf16(log_1m)
        after = _dot(later, hi) + _dot(later, lo) + tail_ref[...]
        a = jnp.exp(z - sp + after)
        if mask is not None:
            a = jnp.where(mask, a, 0.0)
        a_t = a.T[:n_cols].astype(BF)
        acc_ref[...] += _dot(a_t, vblk.astype(BF))
        tail_ref[...] += jnp.sum(log_1m, axis=0, keepdims=True)

    @pl.when(j == 0)
    def _():
        tail_ref[...] = jnp.zeros_like(tail_ref)
        acc_ref[...] = jnp.zeros_like(acc_ref)
        nk = kn_ref.shape[1]
        r = lax.broadcasted_iota(I32, (nk, 128), 0)
        c = lax.broadcasted_iota(I32, (nk, 128), 1)
        block(kn_ref[0], vn_ref[0], (r < c % n_q) & (r < n_new))

    for g in range(pages):
        block(k_refs[g][0], v_refs[g][0], None)

    @pl.when(j == pl.num_programs(1) - 1)
    def _():
        for h in range(o_ref.shape[-1] // HEAD_DIM):
            sl = slice(h * HEAD_DIM, (h + 1) * HEAD_DIM)
            o_ref[0, :, sl] = acc_ref[h * n_q:(h + 1) * n_q, sl].astype(o_ref.dtype)


def _stick_breaking_decode(q, k_new, v_new, cache_k, cache_v, page_table, bias, pages):
    b, t, main_w = q.shape
    n_heads = main_w // HEAD_DIM
    n_pages = page_table.shape[1]
    page = cache_k.shape[1]
    n_cols = n_heads * t
    qh = q.reshape(b, t, n_heads, HEAD_DIM).astype(F32)
    eye = jnp.eye(n_heads, dtype=F32)
    qt = jnp.einsum("bihd,hg->bhdgi", qh, eye).reshape(b, main_w, n_cols)
    qt = jnp.pad(qt, ((0, 0), (0, 0), (0, 128 - n_cols))).astype(BF)
    bias_c = jnp.pad(jnp.repeat(bias.astype(F32), t), (0, 128 - n_cols)).reshape(1, 128)
    new_rows = page
    k_pad = jnp.pad(k_new, ((0, 0), (0, new_rows - t), (0, 0)))
    v_pad = jnp.pad(v_new, ((0, 0), (0, new_rows - t), (0, 0)))

    def page_map(g):
        return lambda i, j, pt: (pt[i, n_pages - 1 - (j * pages + g)], 0, 0)

    cache_specs = [pl.BlockSpec((1, page, main_w), page_map(g)) for g in range(pages)]
    grid_spec = pltpu.PrefetchScalarGridSpec(
        num_scalar_prefetch=1,
        grid=(b, n_pages // pages),
        in_specs=[pl.BlockSpec((1, main_w, 128), lambda i, j, pt: (i, 0, 0)),
                  pl.BlockSpec((1, 128), lambda i, j, pt: (0, 0)),
                  pl.BlockSpec((1, new_rows, main_w), lambda i, j, pt: (i, 0, 0)),
                  pl.BlockSpec((1, new_rows, main_w), lambda i, j, pt: (i, 0, 0))]
                 + cache_specs + cache_specs,
        out_specs=pl.BlockSpec((1, t, main_w), lambda i, j, pt: (i, 0, 0)),
        scratch_shapes=[pltpu.VMEM((1, 128), F32), pltpu.VMEM((n_cols, main_w), F32)],
    )
    return pl.pallas_call(
        functools.partial(_sb_dec_kernel, pages=pages, n_q=t, n_new=t),
        grid_spec=grid_spec,
        out_shape=jax.ShapeDtypeStruct((b, t, main_w), BF),
        compiler_params=_params(("parallel", "arbitrary")),
        name="stick_breaking_decode",
    )(page_table, qt, bias_c, k_pad, v_pad, *([cache_k] * pages), *([cache_v] * pages))


def _out_ln_kernel(ym_ref, ye_ref, x_ref, wa_ref, wb_ref, g_ref, b_ref, rh_ref, rl_ref,
                   x1_ref, lt_ref, *, alpha):
    h = _dot(ym_ref[...], wa_ref[...]) + _dot(ye_ref[...], wb_ref[...])
    x1 = _layer_norm(alpha * x_ref[...] + h, g_ref[...], b_ref[...])
    x1_ref[...] = x1
    xh, xl = _split_bf16(x1)
    rh = rh_ref[...]
    lt = (lax.dot_general(rh, xh, _NT, preferred_element_type=F32)
          + lax.dot_general(rh, xl, _NT, preferred_element_type=F32)
          + lax.dot_general(rl_ref[...], xh, _NT, preferred_element_type=F32))
    lt_ref[...] = lt


def _out_ln_route(y_main, y_mem, x, w_out_bf, gamma, beta, r_hi, r_lo, alpha, tm):
    n, d = x.shape
    main_w = y_main.shape[1]
    mem_w = y_mem.shape[1]
    return pl.pallas_call(
        functools.partial(_out_ln_kernel, alpha=alpha),
        grid=(n // tm,),
        in_specs=[pl.BlockSpec((tm, main_w), lambda i: (i, 0)),
                  pl.BlockSpec((tm, mem_w), lambda i: (i, 0)),
                  pl.BlockSpec((tm, d), lambda i: (i, 0)),
                  pl.BlockSpec((main_w, d), lambda i: (0, 0)),
                  pl.BlockSpec((mem_w, d), lambda i: (main_w // mem_w, 0)),
                  pl.BlockSpec((1, d), lambda i: (0, 0)),
                  pl.BlockSpec((1, d), lambda i: (0, 0)),
                  pl.BlockSpec((ROUTE_ROWS, d), lambda i: (0, 0)),
                  pl.BlockSpec((ROUTE_ROWS, d), lambda i: (0, 0))],
        out_specs=[pl.BlockSpec((tm, d), lambda i: (i, 0)),
                   pl.BlockSpec((ROUTE_ROWS, tm), lambda i: (0, i))],
        out_shape=[jax.ShapeDtypeStruct((n, d), F32),
                   jax.ShapeDtypeStruct((ROUTE_ROWS, n), F32)],
        compiler_params=_params(("parallel",)),
        name="out_ln_route",
    )(y_main, y_mem, x, w_out_bf, w_out_bf, gamma.reshape(1, d), beta.reshape(1, d), r_hi, r_lo)


def _route_kernel(lt_ref, rb_ref, e1_ref, e2_ref, g1_ref, g2_ref):
    lt = lt_ref[...] + rb_ref[...]
    ng, ne = N_GROUPS, EXP_PER_GROUP
    lg = [lt[g:g + 1, :] for g in range(ng)]
    m = lg[0]
    gi = jnp.zeros(m.shape, I32)
    for g in range(1, ng):
        better = lg[g] > m
        gi = jnp.where(better, g, gi)
        m = jnp.where(better, lg[g], m)
    den = jnp.exp(lg[0] - m)
    for g in range(1, ng):
        den = den + jnp.exp(lg[g] - m)
    p_sel = 1.0 / den
    le = []
    for e in range(ne):
        v = lt[ng + e:ng + e + 1, :]
        for g in range(1, ng):
            r = ng + g * ne + e
            v = jnp.where(gi == g, lt[r:r + 1, :], v)
        le.append(v)
    v1 = le[0]
    i1 = jnp.zeros(m.shape, I32)
    for e in range(1, ne):
        better = le[e] > v1
        i1 = jnp.where(better, e, i1)
        v1 = jnp.where(better, le[e], v1)
    v2 = jnp.full(m.shape, -jnp.inf, F32)
    i2 = jnp.zeros(m.shape, I32)
    for e in range(ne):
        better = (i1 != e) & (le[e] > v2)
        i2 = jnp.where(better, e, i2)
        v2 = jnp.where(better, le[e], v2)
    t = jnp.exp(v2 - v1)
    w_a = 1.0 / (1.0 + t)
    w_b = t / (1.0 + t)
    e1_ref[...] = gi * ne + i1
    e2_ref[...] = gi * ne + i2
    g1_ref[...] = p_sel * w_a
    g2_ref[...] = p_sel * w_b


def _route(lt, rbias):
    n = lt.shape[1]
    shp = lambda dt: jax.ShapeDtypeStruct((1, n), dt)
    return pl.pallas_call(
        _route_kernel,
        out_shape=[shp(I32), shp(I32), shp(F32), shp(F32)],
        compiler_params=pltpu.CompilerParams(vmem_limit_bytes=VMEM_LIMIT),
        name="route",
    )(lt, rbias)


def _moe_plan(e1, e2, tm, n_tiles):
    n = e1.shape[0]
    e = jnp.concatenate([e1, e2])
    oh = (e[:, None] == jnp.arange(N_EXPERTS, dtype=I32)[None, :]).astype(I32)
    csum = jnp.cumsum(oh, axis=0)
    rank = jnp.sum((csum - oh) * oh, axis=1)
    counts = csum[-1]
    tiles = (counts + tm - 1) // tm
    tile_end = jnp.cumsum(tiles)
    starts = (tile_end - tiles) * tm
    dest = jnp.sum(oh * starts[None, :], axis=1) + rank
    n_used = tile_end[-1]
    tidx = jnp.arange(n_tiles, dtype=I32)
    te = jnp.sum((tidx[:, None] >= tile_end[None, :]).astype(I32), axis=1)
    last_e = jnp.max(jnp.where(tiles > 0, jnp.arange(N_EXPERTS, dtype=I32), 0))
    te = jnp.where(tidx < n_used, te, last_e).astype(I32)
    tok = jnp.concatenate([jnp.arange(n, dtype=I32)] * 2)
    src = jnp.zeros((n_tiles * tm,), I32).at[dest].set(tok)
    return dest[:n].astype(I32), dest[n:].astype(I32), src, te, n_used.reshape(1).astype(I32)


def _moe_kernel(te_ref, src_ref, nu_ref, x_hbm, wg_ref, wu_ref, wd_ref, y_ref,
                xbuf, sem, wg_s, wu_s, wd_s, *, tm):
    j = pl.program_id(0)
    n_used = nu_ref[0]

    def row_copy(tile, r, slot):
        row = src_ref[tile * tm + r]
        return pltpu.make_async_copy(x_hbm.at[pl.ds(row, 1)], xbuf.at[slot, pl.ds(r, 1)],
                                     sem.at[slot])

    def start_tile(tile, slot):
        def body(r, c):
            row_copy(tile, r, slot).start()
            return c
        lax.fori_loop(0, tm, body, 0)

    def wait_tile(tile, slot):
        def body(r, c):
            row_copy(tile, r, slot).wait()
            return c
        lax.fori_loop(0, tm, body, 0)

    @pl.when(j == 0)
    def _():
        start_tile(0, 0)

    @pl.when(j + 1 < n_used)
    def _():
        start_tile(j + 1, (j + 1) % 2)

    e = te_ref[j]
    e_prev = te_ref[jnp.maximum(j - 1, 0)]

    @pl.when((j == 0) | (e != e_prev))
    def _():
        wg_s[...] = wg_ref[0].astype(BF)
        wu_s[...] = wu_ref[0].astype(BF)
        wd_s[...] = wd_ref[0].astype(BF)

    @pl.when(j < n_used)
    def _():
        slot = j % 2
        wait_tile(j, slot)
        x = xbuf[slot].astype(BF)
        g = _dot(x, wg_s[...])
        u = _dot(x, wu_s[...])
        h = (g * jax.nn.sigmoid(g) * u).astype(BF)
        y_ref[...] = _dot(h, wd_s[...])

    @pl.when(j >= n_used)
    def _():
        y_ref[...] = jnp.zeros_like(y_ref)


def _moe_experts(x1, src, te, n_used, w_gate, w_up, w_down, tm, n_tiles):
    n, d = x1.shape
    ff = w_gate.shape[-1]
    grid_spec = pltpu.PrefetchScalarGridSpec(
        num_scalar_prefetch=3,
        grid=(n_tiles,),
        in_specs=[pl.BlockSpec(memory_space=pl.ANY),
                  pl.BlockSpec((1, d, ff), lambda j, te, src, nu: (te[j], 0, 0)),
                  pl.BlockSpec((1, d, ff), lambda j, te, src, nu: (te[j], 0, 0)),
                  pl.BlockSpec((1, ff, d), lambda j, te, src, nu: (te[j], 0, 0))],
        out_specs=pl.BlockSpec((tm, d), lambda j, te, src, nu: (j, 0)),
        scratch_shapes=[pltpu.VMEM((2, tm, d), F32),
                        pltpu.SemaphoreType.DMA((2,)),
                        pltpu.VMEM((d, ff), BF),
                        pltpu.VMEM((d, ff), BF),
                        pltpu.VMEM((ff, d), BF)],
    )
    return pl.pallas_call(
        functools.partial(_moe_kernel, tm=tm),
        grid_spec=grid_spec,
        out_shape=jax.ShapeDtypeStruct((n_tiles * tm, d), F32),
        compiler_params=_params(("arbitrary",)),
        name="moe_experts",
    )(te, src, n_used, x1, w_gate, w_up, w_down)


def _combine_kernel(d1_ref, d2_ref, x1_ref, g1_ref, g2_ref, gam_ref, bet_ref, y_hbm,
                    x2_ref, x2b_ref, buf, sem, *, tm, alpha):
    i = pl.program_id(0)
    n_steps = pl.num_programs(0)

    def row_copy(step, r, slot, which):
        d_ref = d1_ref if which == 0 else d2_ref
        row = d_ref[step * tm + r]
        return pltpu.make_async_copy(y_hbm.at[pl.ds(row, 1)], buf.at[slot, which, pl.ds(r, 1)],
                                     sem.at[slot, which])

    def start_step(step, slot):
        def body(r, c):
            row_copy(step, r, slot, 0).start()
            row_copy(step, r, slot, 1).start()
            return c
        lax.fori_loop(0, tm, body, 0)

    def wait_step(step, slot):
        def body(r, c):
            row_copy(step, r, slot, 0).wait()
            row_copy(step, r, slot, 1).wait()
            return c
        lax.fori_loop(0, tm, body, 0)

    @pl.when(i == 0)
    def _():
        start_step(0, 0)

    @pl.when(i + 1 < n_steps)
    def _():
        start_step(i + 1, (i + 1) % 2)

    slot = i % 2
    wait_step(i, slot)
    m = g1_ref[...] * buf[slot, 0] + g2_ref[...] * buf[slot, 1]
    x2 = _layer_norm(alpha * x1_ref[...] + m, gam_ref[...], bet_ref[...])
    x2_ref[...] = x2
    x2b_ref[...] = x2.astype(BF)


def _combine_ln(x1, y_rows, d1, d2, g1, g2, gamma, beta, alpha, tm):
    n, d = x1.shape
    grid_spec = pltpu.PrefetchScalarGridSpec(
        num_scalar_prefetch=2,
        grid=(n // tm,),
        in_specs=[pl.BlockSpec((tm, d), lambda i, a, b: (i, 0)),
                  pl.BlockSpec((tm, 1), lambda i, a, b: (i, 0)),
                  pl.BlockSpec((tm, 1), lambda i, a, b: (i, 0)),
                  pl.BlockSpec((1, d), lambda i, a, b: (0, 0)),
                  pl.BlockSpec((1, d), lambda i, a, b: (0, 0)),
                  pl.BlockSpec(memory_space=pl.ANY)],
        out_specs=[pl.BlockSpec((tm, d), lambda i, a, b: (i, 0)),
                   pl.BlockSpec((tm, d), lambda i, a, b: (i, 0))],
        scratch_shapes=[pltpu.VMEM((2, 2, tm, d), F32),
                        pltpu.SemaphoreType.DMA((2, 2))],
    )
    return pl.pallas_call(
        functools.partial(_combine_kernel, tm=tm, alpha=alpha),
        grid_spec=grid_spec,
        out_shape=[jax.ShapeDtypeStruct((n, d), F32), jax.ShapeDtypeStruct((n, d), BF)],
        compiler_params=_params(("arbitrary",)),
        name="combine_ln",
    )(d1, d2, x1, g1.reshape(n, 1), g2.reshape(n, 1), gamma.reshape(1, d), beta.reshape(1, d),
      y_rows)


def _hier_moe_ln(x1, lt, rbias, w_gate, w_up, w_down, gamma, beta, alpha, tm_moe, tm_tok):
    n = x1.shape[0]
    e1, e2, g1, g2 = _route(lt, rbias)
    n_tiles = -(-(2 * n + N_EXPERTS * (tm_moe - 1)) // tm_moe)
    d1, d2, src, te, n_used = _moe_plan(e1[0], e2[0], tm_moe, n_tiles)
    y_rows = _moe_experts(x1, src, te, n_used, w_gate, w_up, w_down, tm_moe, n_tiles)
    return _combine_ln(x1, y_rows, d1, d2, g1[0], g2[0], gamma, beta, alpha, tm_tok)


def _trunk(x3, pool_prev, mem_k, mem_v, sb_past, p, *, decode):
    b, t, d = x3.shape
    depth = p["w_in"].shape[0]
    n_a = p["w_pool"].shape[0]
    main_w = p["w_sb_kv"].shape[1] // 2
    mem_w = d - main_w
    n_heads = main_w // HEAD_DIM
    alpha = float((2 * depth) ** 0.25)
    n = b * t
    tm = min(512, n)
    tm_c = min(256, n)
    tm_moe = 256 if n >= 2048 else 16
    pos0 = 0 if not decode else sb_past[2].shape[1] * sb_past[0].shape[1]

    x = x3.reshape(n, d)
    xb = x
    new_pool = []
    k_bf = v_bf = k_f = v_f = None
    for l in range(depth):
        (proj,) = _matmul(xb, p["w_in"][l], [(0, d, F32)], tm, "proj_in")
        proj3 = proj.reshape(b, t, d)
        main = proj3[..., :main_w]
        if l < n_a:
            if decode:
                u_tm = jnp.transpose(main, (1, 0, 2))
                prev_tm = jnp.transpose(pool_prev[l], (1, 0, 2))
                y_tm = _pool_mix_decode(u_tm, prev_tm, p["w_pool"][l], p["pool_scale"][l], pos0)
                y_main = jnp.transpose(y_tm, (1, 0, 2))
            else:
                y_main = _pool_mix(proj3, pool_prev[l], p["w_pool"][l], p["pool_scale"][l],
                                   main_w, min(256, t), pos0)
            if t >= POOL_STATE:
                new_pool.append(main[:, t - POOL_STATE:, :])
            else:
                new_pool.append(jnp.concatenate([pool_prev[l][:, t:, :], main], axis=1))
        else:
            bias = p["sb_bias"][l - n_a]
            if decode:
                y_main = _stick_breaking_decode(main, k_f.reshape(b, t, main_w),
                                                v_f.reshape(b, t, main_w),
                                                sb_past[0], sb_past[1], sb_past[2], bias, pages=4)
            else:
                y_main = _stick_breaking_prompt(proj3, k_bf.reshape(b, t, main_w),
                                                v_bf.reshape(b, t, main_w), bias, n_heads, 256)
        if decode:
            t_pad = 8
            proj_pad = jnp.pad(proj3, ((0, 0), (0, t_pad - t), (0, 0)))
            y_mem = _mem_attend(proj_pad, mem_k[l], mem_v[l], mem_w, t_pad)[:, :t]
        else:
            y_mem = _mem_attend(proj3, mem_k[l], mem_v[l], mem_w, min(512, t))
        x1, lt = _out_ln_route(y_main.reshape(n, main_w), y_mem.reshape(n, mem_w), x,
                               p["w_out"][l], p["ln1_g"][l], p["ln1_b"][l],
                               p["r_hi"][l], p["r_lo"][l], alpha, tm_c)
        x, xb = _hier_moe_ln(x1, lt, p["r_bias"][l], p["w_gate"][l], p["w_up"][l],
                             p["w_down"][l], p["ln2_g"][l], p["ln2_b"][l], alpha, tm_moe, tm_c)
        if l == n_a - 1:
            k_f, v_f, k_bf, v_bf = _matmul(
                xb, p["w_sb_kv"], [(0, main_w, F32), (main_w, main_w, F32), (0, main_w, BF),
                                   (main_w, main_w, BF)], min(256, n), "proj_sb_kv")
    return x.reshape(b, t, d), jnp.stack(new_pool, axis=0), k_f, v_f


def kernel(x_prompt, x_sample, state_pool, cache_sb_k, cache_sb_v, cache_mem_k, cache_mem_v,
           page_table, mem_prompt, w_in, w_out, w_pool_grp, pool_scale, w_mem_k, w_mem_v,
           ln1_g, ln1_b, ln2_g, ln2_b, w_route_grp, b_route_grp, w_route_exp, b_route_exp,
           w_gate, w_up, w_down, w_sb_k, w_sb_v, sb_bias):
    depth, d, _ = w_in.shape
    bp, _, _ = x_prompt.shape
    bs, ts, _ = x_sample.shape
    n_a = w_pool_grp.shape[0]
    main_w = w_sb_k.shape[1]
    mem_w = w_mem_k.shape[2]
    n_mem = mem_prompt.shape[1]
    n_heads = main_w // HEAD_DIM
    mem_heads = mem_w // HEAD_DIM

    wr = jnp.concatenate([w_route_grp, w_route_exp.reshape(depth, d, N_EXPERTS)], axis=-1)
    wr_t = jnp.pad(jnp.transpose(wr, (0, 2, 1)),
                   ((0, 0), (0, ROUTE_ROWS - N_GROUPS - N_EXPERTS), (0, 0)))
    r_hi = wr_t.astype(BF)
    r_lo = (wr_t - r_hi.astype(F32)).astype(BF)
    r_bias = jnp.pad(jnp.concatenate([b_route_grp, b_route_exp.reshape(depth, N_EXPERTS)], axis=-1),
                     ((0, 0), (0, ROUTE_ROWS - N_GROUPS - N_EXPERTS))).reshape(depth, ROUTE_ROWS, 1)
    p = dict(
        w_in=w_in.astype(BF), w_out=w_out.astype(BF), w_pool=w_pool_grp.astype(BF),
        pool_scale=pool_scale, w_sb_kv=jnp.concatenate([w_sb_k, w_sb_v], axis=1).astype(BF),
        ln1_g=ln1_g, ln1_b=ln1_b, ln2_g=ln2_g, ln2_b=ln2_b, r_hi=r_hi, r_lo=r_lo, r_bias=r_bias,
        w_gate=w_gate, w_up=w_up, w_down=w_down, sb_bias=sb_bias)

    w_mem = jnp.concatenate([jnp.transpose(w_mem_k, (1, 0, 2)).reshape(d, depth * mem_w),
                             jnp.transpose(w_mem_v, (1, 0, 2)).reshape(d, depth * mem_w)],
                            axis=1).astype(BF)
    mem_kv = _mem_project(mem_prompt.reshape(bp * n_mem, d).astype(BF), w_mem, mem_w)
    mem_kv = mem_kv.reshape(2, depth, bp, n_mem, mem_w)
    pool0 = jnp.zeros((n_a, bp, POOL_STATE, main_w), x_prompt.dtype)
    y_p, pool_p, k_p, v_p = _trunk(x_prompt, pool0, mem_kv[0], mem_kv[1], None, p, decode=False)

    n_phys, page = cache_sb_k.shape[:2]
    sb_past = (cache_sb_k.reshape(n_phys, page, main_w), cache_sb_v.reshape(n_phys, page, main_w),
               page_table)
    y_s, pool_s, k_s, v_s = _trunk(
        x_sample, state_pool, cache_mem_k.reshape(depth, bs, n_mem, mem_w),
        cache_mem_v.reshape(depth, bs, n_mem, mem_w), sb_past, p, decode=True)

    tp = x_prompt.shape[1]
    return (y_p, y_s, pool_p, pool_s,
            k_p.reshape(bp, tp, n_heads, HEAD_DIM), v_p.reshape(bp, tp, n_heads, HEAD_DIM),
            k_s.reshape(bs, ts, n_heads, HEAD_DIM), v_s.reshape(bs, ts, n_heads, HEAD_DIM),
            mem_kv[0].reshape(depth, bp, n_mem, mem_heads, HEAD_DIM),
            mem_kv[1].reshape(depth, bp, n_mem, mem_heads, HEAD_DIM))
```

```python
import functools

import jax
import jax.numpy as jnp
import numpy as np
from jax import lax
from jax.experimental import pallas as pl
from jax.experimental.pallas import tpu as pltpu

F32 = jnp.float32
BF = jnp.bfloat16
I32 = jnp.int32

HEAD_DIM = 128
POOL_WINDOWS = (2, 4, 8, 16)
POOL_STATE = max(POOL_WINDOWS) - 1
N_GROUPS = 4
EXP_PER_GROUP = 4
N_EXPERTS = N_GROUPS * EXP_PER_GROUP
LN_EPS = 1e-5
ROUTE_ROWS = 32
VMEM_LIMIT = 56 * 2**20

_NT = (((1,), (1,)), ((), ()))


def _params(sem):
    return pltpu.CompilerParams(dimension_semantics=sem, vmem_limit_bytes=VMEM_LIMIT)


def _dot(a, b):
    return jnp.dot(a, b, preferred_element_type=F32)


def _layer_norm(v, g, b):
    mu = jnp.mean(v, axis=-1, keepdims=True)
    c = v - mu
    var = jnp.mean(c * c, axis=-1, keepdims=True)
    return c * lax.rsqrt(var + LN_EPS) * g + b


def _softplus(z):
    return jnp.maximum(z, 0.0) + jnp.log(1.0 + jnp.exp(jnp.minimum(z, -z)))


def _split_bf16(x):
    hi = x.astype(BF)
    lo = (x - hi.astype(F32)).astype(BF)
    return hi, lo


def _mm_kernel(x_ref, w_ref, *o_refs, offsets):
    acc = _dot(x_ref[...].astype(BF), w_ref[...])
    for o_ref, off in zip(o_refs, offsets):
        o_ref[...] = acc[:, off:off + o_ref.shape[-1]].astype(o_ref.dtype)


def _matmul(x, w, layer, outs, tm, name):
    m, k = x.shape
    n = w.shape[-1]
    return pl.pallas_call(
        functools.partial(_mm_kernel, offsets=tuple(o for o, _, _ in outs)),
        grid=(m // tm,),
        in_specs=[pl.BlockSpec((tm, k), lambda i: (i, 0)),
                  pl.BlockSpec((None, k, n), lambda i: (layer, 0, 0))],
        out_specs=[pl.BlockSpec((tm, c), lambda i: (i, 0)) for _, c, _ in outs],
        out_shape=[jax.ShapeDtypeStruct((m, c), dt) for _, c, dt in outs],
        compiler_params=_params(("parallel",)),
        name=name,
    )(x, w)


def _memproj_kernel(x_ref, w_ref, o_ref):
    o_ref[0] = _dot(x_ref[...], w_ref[...])


def _mem_project(mem_bf, w_cat, width):
    m, d = mem_bf.shape
    nj = w_cat.shape[1] // width
    return pl.pallas_call(
        _memproj_kernel,
        grid=(nj,),
        in_specs=[pl.BlockSpec((m, d), lambda j: (0, 0)),
                  pl.BlockSpec((d, width), lambda j: (0, j))],
        out_specs=pl.BlockSpec((1, m, width), lambda j: (j, 0, 0)),
        out_shape=jax.ShapeDtypeStruct((nj, m, width), F32),
        compiler_params=_params(("parallel",)),
        name="mem_project",
    )(mem_bf, w_cat)


def _pool_kernel(u_ref, prev_ref, w_ref, sc_ref, y_ref, buf, *, tt, nt, pos0):
    t = pl.program_id(1)
    p = POOL_STATE
    cg = w_ref.shape[-1]

    @pl.when(t == 0)
    def _():
        buf[1:1 + p, :] = prev_ref[0]

    u = u_ref[0]
    buf[p + 1:p + 1 + tt, :] = u
    pos = lax.broadcasted_iota(I32, (tt, 1), 0) + (t * tt + pos0)
    for g, w in enumerate(POOL_WINDOWS):
        c0 = g * cg
        ug = u[:, c0:c0 + cg]
        s = ug
        for k in range(1, w):
            s = s + buf[p + 1 - k:p + 1 - k + tt, c0:c0 + cg]
        cnt = jnp.minimum(w, pos + 1).astype(F32)
        d = (s / cnt - ug).astype(BF)
        yg = _dot(d, w_ref[g]) * sc_ref[:, c0:c0 + cg]
        y_ref[0, :, c0:c0 + cg] = yg.astype(y_ref.dtype)
    if nt > 1:
        buf[1:1 + p, :] = buf[tt + 1:tt + 1 + p, :]


def _pool_mix(proj, prev, w_grp_bf, scale, main_w, tt, pos0):
    b, t, _ = proj.shape
    nt = t // tt
    cg = main_w // len(POOL_WINDOWS)
    return pl.pallas_call(
        functools.partial(_pool_kernel, tt=tt, nt=nt, pos0=pos0),
        grid=(b, nt),
        in_specs=[pl.BlockSpec((1, tt, main_w), lambda i, j: (i, j, 0)),
                  pl.BlockSpec((1, POOL_STATE, main_w), lambda i, j: (i, 0, 0)),
                  pl.BlockSpec((len(POOL_WINDOWS), cg, cg), lambda i, j: (0, 0, 0)),
                  pl.BlockSpec((1, main_w), lambda i, j: (0, 0))],
        out_specs=pl.BlockSpec((1, tt, main_w), lambda i, j: (i, j, 0)),
        out_shape=jax.ShapeDtypeStruct((b, t, main_w), BF),
        scratch_shapes=[pltpu.VMEM((POOL_STATE + 1 + tt, main_w), F32)],
        compiler_params=_params(("parallel", "arbitrary")),
        name="pool_mix",
    )(proj, prev, w_grp_bf, scale.reshape(1, main_w))


def _pool_dec_kernel(u_ref, prev_ref, w_ref, sc_ref, y_ref, *, pos0):
    p = POOL_STATE
    nt = u_ref.shape[0]
    cg = w_ref.shape[-1]
    cat = [prev_ref[k] for k in range(p)] + [u_ref[i] for i in range(nt)]
    for g, w in enumerate(POOL_WINDOWS):
        c0 = g * cg
        ds = []
        for i in range(nt):
            s = cat[p + i][:, c0:c0 + cg]
            for k in range(1, w):
                s = s + cat[p + i - k][:, c0:c0 + cg]
            cnt = float(min(w, pos0 + i + 1))
            ds.append(s / cnt - cat[p + i][:, c0:c0 + cg])
        d = jnp.concatenate(ds, axis=0).astype(BF)
        yg = _dot(d, w_ref[g]) * sc_ref[:, c0:c0 + cg]
        nb = ds[0].shape[0]
        for i in range(nt):
            y_ref[i, :, c0:c0 + cg] = yg[i * nb:(i + 1) * nb].astype(y_ref.dtype)


def _pool_mix_decode(u_tm, prev_tm, w_grp_bf, scale, pos0):
    nt, b, c = u_tm.shape
    return pl.pallas_call(
        functools.partial(_pool_dec_kernel, pos0=pos0),
        out_shape=jax.ShapeDtypeStruct((nt, b, c), BF),
        compiler_params=pltpu.CompilerParams(vmem_limit_bytes=VMEM_LIMIT),
        name="pool_mix_decode",
    )(u_tm, prev_tm, w_grp_bf, scale.reshape(1, c))


def _mem_kernel(q_ref, k_ref, v_ref, o_ref, *, cdt):
    nh = q_ref.shape[-1] // HEAD_DIM
    scale = HEAD_DIM ** -0.5
    for h in range(nh):
        sl = slice(h * HEAD_DIM, (h + 1) * HEAD_DIM)
        q = q_ref[0, :, sl].astype(cdt)
        k = k_ref[0, :, sl].astype(cdt)
        v = v_ref[0, :, sl].astype(cdt)
        s = lax.dot_general(q, k, _NT, preferred_element_type=F32) * scale
        e = jnp.exp(s - jnp.max(s, axis=-1, keepdims=True))
        den = jnp.sum(e, axis=-1, keepdims=True)
        o = _dot(e.astype(cdt), v) / den
        o_ref[0, :, sl] = o.astype(o_ref.dtype)


def _mem_attend(proj, mk, mv, mem_w, tt):
    b, t, mix_w = proj.shape
    n_mem = mk.shape[1]
    qblk = (mix_w - mem_w) // mem_w
    cdt = BF if tt % 16 == 0 else F32
    return pl.pallas_call(
        functools.partial(_mem_kernel, cdt=cdt),
        grid=(b, t // tt),
        in_specs=[pl.BlockSpec((1, tt, mem_w), lambda i, j: (i, j, qblk)),
                  pl.BlockSpec((1, n_mem, mem_w), lambda i, j: (i, 0, 0)),
                  pl.BlockSpec((1, n_mem, mem_w), lambda i, j: (i, 0, 0))],
        out_specs=pl.BlockSpec((1, tt, mem_w), lambda i, j: (i, j, 0)),
        out_shape=jax.ShapeDtypeStruct((b, t, mem_w), BF),
        compiler_params=_params(("parallel", "parallel")),
        name="mem_attend",
    )(proj, mk, mv)


def _sb_kernel(q_ref, k_ref, v_ref, bias_ref, o_ref, *, tq, hp):
    qi = pl.program_id(2)
    inv_sqrt = 1.0 / np.sqrt(np.float32(HEAD_DIM))
    jj = lax.broadcasted_iota(I32, (tq, tq), 0)
    ss = lax.broadcasted_iota(I32, (tq, tq), 1)
    later = (jj > ss).astype(BF)
    later2 = jnp.concatenate([later, later], axis=0)
    causal = ss < jj
    heads = [slice(h * HEAD_DIM, (h + 1) * HEAD_DIM) for h in range(hp)]
    qs = [(q_ref[0, :, sl] * inv_sqrt).astype(BF) for sl in heads]
    biases = [bias_ref[0, h:h + 1, :] for h in range(hp)]

    def blocks(start, state, mask):
        rng = range(hp)
        zs = [lax.dot_general(qs[h], k_ref[0, pl.ds(start, tq), heads[h]], _NT,
                              preferred_element_type=F32) + biases[h] for h in rng]
        sps = [_softplus(z) for z in zs]
        if mask is not None:
            sps = [jnp.where(mask, sp, 0.0) for sp in sps]
        splits = [_split_bf16(sp) for sp in sps]
        later_sums = [_dot(jnp.concatenate([hi, lo], axis=1), later2) for hi, lo in splits]
        ws = [jnp.exp(zs[h] - sps[h] - later_sums[h] + state[h][0]) for h in rng]
        if mask is not None:
            ws = [jnp.where(mask, a, 0.0) for a in ws]
        accs = [state[h][1] + _dot(ws[h].astype(BF), v_ref[0, pl.ds(start, tq), heads[h]])
                for h in rng]
        tails = [state[h][0] - jnp.sum(sps[h], axis=1, keepdims=True) for h in rng]
        return tuple(zip(tails, accs))

    diag = pl.multiple_of(qi * tq, tq)
    zero = (jnp.zeros((tq, 1), F32), jnp.zeros((tq, HEAD_DIM), F32))
    state = blocks(diag, (zero,) * hp, causal)

    def body(n, carry):
        return blocks(pl.multiple_of((qi - 1 - n) * tq, tq), carry, None)

    state = lax.fori_loop(0, qi, body, state)
    for h in range(hp):
        o_ref[0, :, heads[h]] = state[h][1].astype(o_ref.dtype)


def _stick_breaking_prompt(proj, k_bf, v_bf, bias, n_heads, tq, hp):
    b, t, _ = proj.shape
    main_w = n_heads * HEAD_DIM
    w = hp * HEAD_DIM
    bias_b = jnp.broadcast_to(bias.astype(F32).reshape(n_heads // hp, hp, 1),
                              (n_heads // hp, hp, tq))
    return pl.pallas_call(
        functools.partial(_sb_kernel, tq=tq, hp=hp),
        grid=(b, n_heads // hp, t // tq),
        in_specs=[pl.BlockSpec((1, tq, w), lambda i, h, j: (i, j, h)),
                  pl.BlockSpec((1, t, w), lambda i, h, j: (i, 0, h)),
                  pl.BlockSpec((1, t, w), lambda i, h, j: (i, 0, h)),
                  pl.BlockSpec((1, hp, tq), lambda i, h, j: (h, 0, 0))],
        out_specs=pl.BlockSpec((1, tq, w), lambda i, h, j: (i, j, h)),
        out_shape=jax.ShapeDtypeStruct((b, t, main_w), BF),
        compiler_params=_params(("parallel", "parallel", "arbitrary")),
        name="stick_breaking_prompt",
    )(proj, k_bf, v_bf, bias_b)


def _sb_dec_kernel(pt_ref, qt_ref, bias_ref, kn_ref, vn_ref, *rest, pages, n_q, n_new):
    k_refs = rest[:pages]
    v_refs = rest[pages:2 * pages]
    o_ref, tail_ref, acc_ref = rest[2 * pages:]
    j = pl.program_id(1)
    inv_sqrt = 1.0 / np.sqrt(np.float32(HEAD_DIM))
    qt = qt_ref[0]
    bias = bias_ref[...]
    n_cols = acc_ref.shape[0]
    nk = kn_ref.shape[1]
    n_heads = k_refs[0].shape[2]
    ss = lax.broadcasted_iota(I32, (nk, nk), 0)
    jj = lax.broadcasted_iota(I32, (nk, nk), 1)
    later = (jj > ss).astype(BF)

    def heads_cat(ref):
        return jnp.concatenate([ref[0, :, h, :] for h in range(n_heads)], axis=1).astype(BF)

    def scores(kcat, mask):
        z = _dot(kcat, qt) * inv_sqrt + bias
        sp = _softplus(z)
        if mask is not None:
            sp = jnp.where(mask, sp, 0.0)
        hi, lo = _split_bf16(sp)
        both = _dot(later, jnp.concatenate([hi, lo], axis=1))
        later_sum = both[:, :128] + both[:, 128:]
        return z, sp, later_sum, jnp.sum(sp, axis=0, keepdims=True)

    def weights(z, sp, later_sum, tail, mask):
        a = jnp.exp(z - sp - later_sum + tail)
        if mask is not None:
            a = jnp.where(mask, a, 0.0)
        return a.T[:n_cols].astype(BF)

    @pl.when(j == 0)
    def _():
        r = lax.broadcasted_iota(I32, (nk, 128), 0)
        c = lax.broadcasted_iota(I32, (nk, 128), 1)
        mask = (r < c % n_q) & (r < n_new)
        z, sp, later_sum, total = scores(kn_ref[0].astype(BF), mask)
        a_t = weights(z, sp, later_sum, jnp.zeros_like(total), mask)
        acc_ref[...] = _dot(a_t, vn_ref[0].astype(BF))
        tail_ref[...] = -total

    parts = [scores(heads_cat(k_refs[g]), None) for g in range(pages)]
    tail = tail_ref[...]
    a_ts = []
    for z, sp, later_sum, total in parts:
        a_ts.append(weights(z, sp, later_sum, tail, None))
        tail = tail - total
    v_all = jnp.concatenate([heads_cat(v_refs[g]) for g in range(pages)], axis=0)
    acc_ref[...] += _dot(jnp.concatenate(a_ts, axis=1), v_all)
    tail_ref[...] = tail

    @pl.when(j == pl.num_programs(1) - 1)
    def _():
        for h in range(o_ref.shape[-1] // HEAD_DIM):
            sl = slice(h * HEAD_DIM, (h + 1) * HEAD_DIM)
            o_ref[0, :, sl] = acc_ref[h * n_q:(h + 1) * n_q, sl].astype(o_ref.dtype)


def _stick_breaking_decode(q, k_new, v_new, cache_k, cache_v, page_table, bias, pages):
    b, t, main_w = q.shape
    n_pages = page_table.shape[1]
    _, page, n_heads, _ = cache_k.shape
    n_cols = n_heads * t
    qh = q.reshape(b, t, n_heads, HEAD_DIM).astype(F32)
    eye = jnp.eye(n_heads, dtype=F32)
    qt = jnp.einsum("bihd,hg->bhdgi", qh, eye).reshape(b, main_w, n_cols)
    qt = jnp.pad(qt, ((0, 0), (0, 0), (0, 128 - n_cols))).astype(BF)
    bias_c = jnp.pad(jnp.repeat(bias.astype(F32), t), (0, 128 - n_cols)).reshape(1, 128)
    new_rows = page
    k_pad = jnp.pad(k_new, ((0, 0), (0, new_rows - t), (0, 0)))
    v_pad = jnp.pad(v_new, ((0, 0), (0, new_rows - t), (0, 0)))

    def page_map(g):
        return lambda i, j, pt: (pt[i, n_pages - 1 - (j * pages + g)], 0, 0, 0)

    cache_specs = [pl.BlockSpec((1, page, n_heads, HEAD_DIM), page_map(g)) for g in range(pages)]
    grid_spec = pltpu.PrefetchScalarGridSpec(
        num_scalar_prefetch=1,
        grid=(b, n_pages // pages),
        in_specs=[pl.BlockSpec((1, main_w, 128), lambda i, j, pt: (i, 0, 0)),
                  pl.BlockSpec((1, 128), lambda i, j, pt: (0, 0)),
                  pl.BlockSpec((1, new_rows, main_w), lambda i, j, pt: (i, 0, 0)),
                  pl.BlockSpec((1, new_rows, main_w), lambda i, j, pt: (i, 0, 0))]
                 + cache_specs + cache_specs,
        out_specs=pl.BlockSpec((1, t, main_w), lambda i, j, pt: (i, 0, 0)),
        scratch_shapes=[pltpu.VMEM((1, 128), F32), pltpu.VMEM((n_cols, main_w), F32)],
    )
    return pl.pallas_call(
        functools.partial(_sb_dec_kernel, pages=pages, n_q=t, n_new=t),
        grid_spec=grid_spec,
        out_shape=jax.ShapeDtypeStruct((b, t, main_w), BF),
        compiler_params=_params(("parallel", "arbitrary")),
        name="stick_breaking_decode",
    )(page_table, qt, bias_c, k_pad, v_pad, *([cache_k] * pages), *([cache_v] * pages))


def _out_ln_kernel(ym_ref, ye_ref, x_ref, wa_ref, wb_ref, g_ref, b_ref, rh_ref, rl_ref,
                   x1_ref, lt_ref, *, alpha):
    h = _dot(ym_ref[...], wa_ref[...]) + _dot(ye_ref[...], wb_ref[...])
    x1 = _layer_norm(alpha * x_ref[...] + h, g_ref[...], b_ref[...])
    x1_ref[...] = x1
    xh, xl = _split_bf16(x1)
    rh = rh_ref[...]
    lt = (lax.dot_general(rh, xh, _NT, preferred_element_type=F32)
          + lax.dot_general(rh, xl, _NT, preferred_element_type=F32)
          + lax.dot_general(rl_ref[...], xh, _NT, preferred_element_type=F32))
    lt_ref[...] = lt


def _out_ln_route(y_main, y_mem, x, w_out_bf, layer, gamma, beta, r_hi, r_lo, alpha, tm):
    n, d = x.shape
    main_w = y_main.shape[1]
    mem_w = y_mem.shape[1]
    return pl.pallas_call(
        functools.partial(_out_ln_kernel, alpha=alpha),
        grid=(n // tm,),
        in_specs=[pl.BlockSpec((tm, main_w), lambda i: (i, 0)),
                  pl.BlockSpec((tm, mem_w), lambda i: (i, 0)),
                  pl.BlockSpec((tm, d), lambda i: (i, 0)),
                  pl.BlockSpec((None, main_w, d), lambda i: (layer, 0, 0)),
                  pl.BlockSpec((None, mem_w, d), lambda i: (layer, main_w // mem_w, 0)),
                  pl.BlockSpec((1, d), lambda i: (0, 0)),
                  pl.BlockSpec((1, d), lambda i: (0, 0)),
                  pl.BlockSpec((ROUTE_ROWS, d), lambda i: (0, 0)),
                  pl.BlockSpec((ROUTE_ROWS, d), lambda i: (0, 0))],
        out_specs=[pl.BlockSpec((tm, d), lambda i: (i, 0)),
                   pl.BlockSpec((ROUTE_ROWS, tm), lambda i: (0, i))],
        out_shape=[jax.ShapeDtypeStruct((n, d), F32),
                   jax.ShapeDtypeStruct((ROUTE_ROWS, n), F32)],
        compiler_params=_params(("parallel",)),
        name="out_ln_route",
    )(y_main, y_mem, x, w_out_bf, w_out_bf, gamma.reshape(1, d), beta.reshape(1, d), r_hi, r_lo)


def _route_kernel(lt_ref, rb_ref, e1_ref, e2_ref, g1_ref, g2_ref):
    lt = lt_ref[...] + rb_ref[...]
    ng, ne = N_GROUPS, EXP_PER_GROUP
    lg = [lt[g:g + 1, :] for g in range(ng)]
    m = lg[0]
    gi = jnp.zeros(m.shape, I32)
    for g in range(1, ng):
        better = lg[g] > m
        gi = jnp.where(better, g, gi)
        m = jnp.where(better, lg[g], m)
    den = jnp.exp(lg[0] - m)
    for g in range(1, ng):
        den = den + jnp.exp(lg[g] - m)
    p_sel = 1.0 / den
    le = []
    for e in range(ne):
        v = lt[ng + e:ng + e + 1, :]
        for g in range(1, ng):
            r = ng + g * ne + e
            v = jnp.where(gi == g, lt[r:r + 1, :], v)
        le.append(v)
    v1 = le[0]
    i1 = jnp.zeros(m.shape, I32)
    for e in range(1, ne):
        better = le[e] > v1
        i1 = jnp.where(better, e, i1)
        v1 = jnp.where(better, le[e], v1)
    v2 = jnp.full(m.shape, -jnp.inf, F32)
    i2 = jnp.zeros(m.shape, I32)
    for e in range(ne):
        better = (i1 != e) & (le[e] > v2)
        i2 = jnp.where(better, e, i2)
        v2 = jnp.where(better, le[e], v2)
    t = jnp.exp(v2 - v1)
    w_a = 1.0 / (1.0 + t)
    w_b = t / (1.0 + t)
    e1_ref[...] = gi * ne + i1
    e2_ref[...] = gi * ne + i2
    g1_ref[...] = p_sel * w_a
    g2_ref[...] = p_sel * w_b


def _route(lt, rbias):
    n = lt.shape[1]
    shp = lambda dt: jax.ShapeDtypeStruct((1, n), dt)
    return pl.pallas_call(
        _route_kernel,
        out_shape=[shp(I32), shp(I32), shp(F32), shp(F32)],
        compiler_params=pltpu.CompilerParams(vmem_limit_bytes=VMEM_LIMIT),
        name="route",
    )(lt, rbias)


def _moe_plan(e1, e2, tm, n_tiles):
    n = e1.shape[0]
    e = jnp.concatenate([e1, e2])
    oh = (e[:, None] == jnp.arange(N_EXPERTS, dtype=I32)[None, :]).astype(I32)
    csum = jnp.cumsum(oh, axis=0)
    rank = jnp.sum((csum - oh) * oh, axis=1)
    counts = csum[-1]
    tiles = (counts + tm - 1) // tm
    tile_end = jnp.cumsum(tiles)
    starts = (tile_end - tiles) * tm
    dest = jnp.sum(oh * starts[None, :], axis=1) + rank
    n_used = tile_end[-1]
    tidx = jnp.arange(n_tiles, dtype=I32)
    te = jnp.sum((tidx[:, None] >= tile_end[None, :]).astype(I32), axis=1)
    last_e = jnp.max(jnp.where(tiles > 0, jnp.arange(N_EXPERTS, dtype=I32), 0))
    te = jnp.where(tidx < n_used, te, last_e).astype(I32)
    rows = n_tiles * tm
    pair_of_row = jnp.full((rows,), -1, I32).at[dest].set(jnp.arange(2 * n, dtype=I32))
    has_pair = pair_of_row >= 0
    filler = jnp.logical_and(~has_pair, jnp.arange(rows, dtype=I32) < n_used * tm)
    spill = 2 * n + tm + jnp.cumsum(filler.astype(I32)) - 1
    src = jnp.where(has_pair, pair_of_row % n, 0).astype(I32)
    dst_rows = jnp.where(has_pair, pair_of_row, jnp.where(filler, spill, 2 * n))
    dst = jnp.concatenate([2 * n + jnp.arange(tm, dtype=I32), dst_rows]).astype(I32)
    return src, dst, te, n_used.reshape(1).astype(I32)


def _moe_kernel(te_ref, src_ref, dst_ref, nu_ref, x_hbm, wg_ref, wu_ref, wd_ref, y_hbm,
                xbuf, ybuf, gsem, ssem, wg_s, wu_s, wd_s, *, tm, n_pairs, n_spill_tiles):
    j = pl.program_id(0)
    n_used = nu_ref[0]
    slot = j % 2
    other = 1 - slot

    def start_gather(tile, to_slot):
        for r in range(tm):
            row = src_ref[tile * tm + r]
            pltpu.make_async_copy(x_hbm.at[pl.ds(row, 1)], xbuf.at[to_slot, pl.ds(r, 1)],
                                  gsem.at[to_slot]).start()

    def wait_gather(of_slot):
        pltpu.make_async_copy(x_hbm.at[pl.ds(0, tm)], xbuf.at[of_slot], gsem.at[of_slot]).wait()

    def start_scatter(tile, from_slot):
        for r in range(tm):
            row = dst_ref[(tile + 1) * tm + r]
            pltpu.make_async_copy(ybuf.at[from_slot, pl.ds(r, 1)], y_hbm.at[pl.ds(row, 1)],
                                  ssem.at[from_slot]).start()

    def wait_scatter(of_slot):
        pltpu.make_async_copy(ybuf.at[of_slot], y_hbm.at[pl.ds(0, tm)], ssem.at[of_slot]).wait()

    @pl.when(j == 0)
    def _():
        start_gather(0, 0)
        ybuf[1] = jnp.zeros(ybuf.shape[1:], F32)
        fills = [pltpu.make_async_copy(ybuf.at[1], y_hbm.at[pl.ds(n_pairs + c * tm, tm)],
                                       ssem.at[1]) for c in range(1, n_spill_tiles)]
        for f in fills:
            f.start()
        for f in fills:
            f.wait()

    e = te_ref[j]
    e_prev = te_ref[jnp.maximum(j - 1, 0)]

    @pl.when((j == 0) | (e != e_prev))
    def _():
        wg_s[...] = wg_ref[0, 0].astype(BF)
        wu_s[...] = wu_ref[0, 0].astype(BF)
        wd_s[...] = wd_ref[0, 0].astype(BF)

    @pl.when((j >= 1) & (j <= n_used))
    def _():
        wait_scatter(slot)

    @pl.when(j < n_used)
    def _():
        wait_gather(slot)
        start_gather(j + 1, other)
        start_scatter(j - 1, other)
        x = xbuf[slot].astype(BF)
        g = _dot(x, wg_s[...])
        u = _dot(x, wu_s[...])
        h = (g * jax.nn.sigmoid(g) * u).astype(BF)
        ybuf[slot] = _dot(h, wd_s[...])

    @pl.when(j == n_used)
    def _():
        wait_gather(slot)
        start_scatter(j - 1, other)
        wait_scatter(other)


def _moe_experts(x1, src, dst, te, n_used, w_gate, w_up, w_down, layer, tm, n_tiles):
    n, d = x1.shape
    ff = w_gate.shape[-1]
    n_spill_tiles = 1 + -(-N_EXPERTS * (tm - 1) // tm)
    grid_spec = pltpu.PrefetchScalarGridSpec(
        num_scalar_prefetch=4,
        grid=(n_tiles,),
        in_specs=[pl.BlockSpec(memory_space=pl.ANY),
                  pl.BlockSpec((1, 1, d, ff), lambda j, te, *_: (layer, te[j], 0, 0)),
                  pl.BlockSpec((1, 1, d, ff), lambda j, te, *_: (layer, te[j], 0, 0)),
                  pl.BlockSpec((1, 1, ff, d), lambda j, te, *_: (layer, te[j], 0, 0))],
        out_specs=pl.BlockSpec(memory_space=pl.ANY),
        scratch_shapes=[pltpu.VMEM((2, tm, d), F32),
                        pltpu.VMEM((2, tm, d), F32),
                        pltpu.SemaphoreType.DMA((2,)),
                        pltpu.SemaphoreType.DMA((2,)),
                        pltpu.VMEM((d, ff), BF),
                        pltpu.VMEM((d, ff), BF),
                        pltpu.VMEM((ff, d), BF)],
    )
    return pl.pallas_call(
        functools.partial(_moe_kernel, tm=tm, n_pairs=2 * n, n_spill_tiles=n_spill_tiles),
        grid_spec=grid_spec,
        out_shape=jax.ShapeDtypeStruct((2 * n + n_spill_tiles * tm, d), F32),
        compiler_params=_params(("arbitrary",)),
        name="moe_experts",
    )(te, src, dst, n_used, x1, w_gate, w_up, w_down)


def _combine_kernel(x1_ref, ya_ref, yb_ref, g1_ref, g2_ref, gam_ref, bet_ref,
                    x2_ref, x2b_ref, *, alpha):
    m = g1_ref[...] * ya_ref[...] + g2_ref[...] * yb_ref[...]
    x2 = _layer_norm(alpha * x1_ref[...] + m, gam_ref[...], bet_ref[...])
    x2_ref[...] = x2
    x2b_ref[...] = x2.astype(BF)


def _combine_ln(x1, y_pairs, g1, g2, gamma, beta, alpha, tm):
    n, d = x1.shape
    nb = n // tm
    return pl.pallas_call(
        functools.partial(_combine_kernel, alpha=alpha),
        grid=(nb,),
        in_specs=[pl.BlockSpec((tm, d), lambda i: (i, 0)),
                  pl.BlockSpec((tm, d), lambda i: (i, 0)),
                  pl.BlockSpec((tm, d), lambda i: (i + nb, 0)),
                  pl.BlockSpec((tm, 1), lambda i: (i, 0)),
                  pl.BlockSpec((tm, 1), lambda i: (i, 0)),
                  pl.BlockSpec((1, d), lambda i: (0, 0)),
                  pl.BlockSpec((1, d), lambda i: (0, 0))],
        out_specs=[pl.BlockSpec((tm, d), lambda i: (i, 0)),
                   pl.BlockSpec((tm, d), lambda i: (i, 0))],
        out_shape=[jax.ShapeDtypeStruct((n, d), F32), jax.ShapeDtypeStruct((n, d), BF)],
        compiler_params=_params(("parallel",)),
        name="combine_ln",
    )(x1, y_pairs, y_pairs, g1.reshape(n, 1), g2.reshape(n, 1), gamma.reshape(1, d),
      beta.reshape(1, d))


def _hier_moe_ln(x1, lt, rbias, w_gate, w_up, w_down, layer, gamma, beta, alpha, tm_moe, tm_tok):
    n = x1.shape[0]
    e1, e2, g1, g2 = _route(lt, rbias)
    n_tiles = (2 * n + N_EXPERTS * (tm_moe - 1)) // tm_moe + 1
    src, dst, te, n_used = _moe_plan(e1[0], e2[0], tm_moe, n_tiles)
    y_pairs = _moe_experts(x1, src, dst, te, n_used, w_gate, w_up, w_down, layer, tm_moe, n_tiles)
    return _combine_ln(x1, y_pairs, g1[0], g2[0], gamma, beta, alpha, tm_tok)


def _trunk(x3, pool_prev, mem_k, mem_v, sb_past, p, *, decode):
    b, t, d = x3.shape
    depth = p["w_in"].shape[0]
    n_a = p["w_pool"].shape[0]
    main_w = p["w_sb_kv"].shape[-1] // 2
    mem_w = d - main_w
    n_heads = main_w // HEAD_DIM
    alpha = float((2 * depth) ** 0.25)
    n = b * t
    tm = min(512, n)
    tm_c = min(256, n)
    tm_moe = 256 if n >= 2048 else 16
    pos0 = 0 if not decode else sb_past[2].shape[1] * sb_past[0].shape[1]

    x = x3.reshape(n, d)
    xb = x
    new_pool = []
    k_bf = v_bf = k_f = v_f = None
    for l in range(depth):
        (proj,) = _matmul(xb, p["w_in"], l, [(0, d, F32)], tm, "proj_in")
        proj3 = proj.reshape(b, t, d)
        main = proj3[..., :main_w]
        if l < n_a:
            if decode:
                u_tm = jnp.transpose(main, (1, 0, 2))
                prev_tm = jnp.transpose(pool_prev[l], (1, 0, 2))
                y_tm = _pool_mix_decode(u_tm, prev_tm, p["w_pool"][l], p["pool_scale"][l], pos0)
                y_main = jnp.transpose(y_tm, (1, 0, 2))
            else:
                y_main = _pool_mix(proj3, pool_prev[l], p["w_pool"][l], p["pool_scale"][l],
                                   main_w, min(256, t), pos0)
            if t >= POOL_STATE:
                new_pool.append(main[:, t - POOL_STATE:, :])
            else:
                new_pool.append(jnp.concatenate([pool_prev[l][:, t:, :], main], axis=1))
        else:
            bias = p["sb_bias"][l - n_a]
            if decode:
                y_main = _stick_breaking_decode(main, k_f.reshape(b, t, main_w),
                                                v_f.reshape(b, t, main_w),
                                                sb_past[0], sb_past[1], sb_past[2], bias, pages=4)
            else:
                y_main = _stick_breaking_prompt(proj3, k_bf.reshape(b, t, main_w),
                                                v_bf.reshape(b, t, main_w), bias, n_heads, 256, 4)
        if decode:
            t_pad = 8
            proj_pad = jnp.pad(proj3, ((0, 0), (0, t_pad - t), (0, 0)))
            y_mem = _mem_attend(proj_pad, mem_k[l], mem_v[l], mem_w, t_pad)[:, :t]
        else:
            y_mem = _mem_attend(proj3, mem_k[l], mem_v[l], mem_w, min(512, t))
        x1, lt = _out_ln_route(y_main.reshape(n, main_w), y_mem.reshape(n, mem_w), x,
                               p["w_out"], l, p["ln1_g"][l], p["ln1_b"][l],
                               p["r_hi"][l], p["r_lo"][l], alpha, tm_c)
        x, xb = _hier_moe_ln(x1, lt, p["r_bias"][l], p["w_gate"], p["w_up"], p["w_down"], l,
                             p["ln2_g"][l], p["ln2_b"][l], alpha, tm_moe, tm)
        if l == n_a - 1:
            k_f, v_f, k_bf, v_bf = _matmul(
                xb, p["w_sb_kv"], 0, [(0, main_w, F32), (main_w, main_w, F32), (0, main_w, BF),
                                      (main_w, main_w, BF)], min(256, n), "proj_sb_kv")
    return x.reshape(b, t, d), jnp.stack(new_pool, axis=0), k_f, v_f


def kernel(x_prompt, x_sample, state_pool, cache_sb_k, cache_sb_v, cache_mem_k, cache_mem_v,
           page_table, mem_prompt, w_in, w_out, w_pool_grp, pool_scale, w_mem_k, w_mem_v,
           ln1_g, ln1_b, ln2_g, ln2_b, w_route_grp, b_route_grp, w_route_exp, b_route_exp,
           w_gate, w_up, w_down, w_sb_k, w_sb_v, sb_bias):
    depth, d, _ = w_in.shape
    bp, _, _ = x_prompt.shape
    bs, ts, _ = x_sample.shape
    n_a = w_pool_grp.shape[0]
    main_w = w_sb_k.shape[1]
    mem_w = w_mem_k.shape[2]
    n_mem = mem_prompt.shape[1]
    n_heads = main_w // HEAD_DIM
    mem_heads = mem_w // HEAD_DIM

    wr = jnp.concatenate([w_route_grp, w_route_exp.reshape(depth, d, N_EXPERTS)], axis=-1)
    wr_t = jnp.pad(jnp.transpose(wr, (0, 2, 1)),
                   ((0, 0), (0, ROUTE_ROWS - N_GROUPS - N_EXPERTS), (0, 0)))
    r_hi = wr_t.astype(BF)
    r_lo = (wr_t - r_hi.astype(F32)).astype(BF)
    r_bias = jnp.pad(jnp.concatenate([b_route_grp, b_route_exp.reshape(depth, N_EXPERTS)], axis=-1),
                     ((0, 0), (0, ROUTE_ROWS - N_GROUPS - N_EXPERTS))).reshape(depth, ROUTE_ROWS, 1)
    p = dict(
        w_in=w_in.astype(BF), w_out=w_out.astype(BF), w_pool=w_pool_grp.astype(BF),
        pool_scale=pool_scale, w_sb_kv=jnp.concatenate([w_sb_k, w_sb_v], axis=1).astype(BF)[None],
        ln1_g=ln1_g, ln1_b=ln1_b, ln2_g=ln2_g, ln2_b=ln2_b, r_hi=r_hi, r_lo=r_lo, r_bias=r_bias,
        w_gate=w_gate, w_up=w_up, w_down=w_down, sb_bias=sb_bias)

    w_mem = jnp.concatenate([jnp.transpose(w_mem_k, (1, 0, 2)).reshape(d, depth * mem_w),
                             jnp.transpose(w_mem_v, (1, 0, 2)).reshape(d, depth * mem_w)],
                            axis=1).astype(BF)
    mem_kv = _mem_project(mem_prompt.reshape(bp * n_mem, d).astype(BF), w_mem, mem_w)
    mem_kv = mem_kv.reshape(2, depth, bp, n_mem, mem_w)
    pool0 = jnp.zeros((n_a, bp, POOL_STATE, main_w), x_prompt.dtype)
    y_p, pool_p, k_p, v_p = _trunk(x_prompt, pool0, mem_kv[0], mem_kv[1], None, p, decode=False)

    sb_past = (cache_sb_k, cache_sb_v, page_table)
    y_s, pool_s, k_s, v_s = _trunk(
        x_sample, state_pool, cache_mem_k.reshape(depth, bs, n_mem, mem_w),
        cache_mem_v.reshape(depth, bs, n_mem, mem_w), sb_past, p, decode=True)

    tp = x_prompt.shape[1]
    return (y_p, y_s, pool_p, pool_s,
            k_p.reshape(bp, tp, n_heads, HEAD_DIM), v_p.reshape(bp, tp, n_heads, HEAD_DIM),
            k_s.reshape(bs, ts, n_heads, HEAD_DIM), v_s.reshape(bs, ts, n_heads, HEAD_DIM),
            mem_kv[0].reshape(depth, bp, n_mem, mem_heads, HEAD_DIM),
            mem_kv[1].reshape(depth, bp, n_mem, mem_heads, HEAD_DIM))
```

```python
import functools

import jax
import jax.numpy as jnp
import numpy as np
from jax import lax
from jax.experimental import pallas as pl
from jax.experimental.pallas import tpu as pltpu

F32 = jnp.float32
BF = jnp.bfloat16
I32 = jnp.int32

HEAD_DIM = 128
POOL_WINDOWS = (2, 4, 8, 16)
POOL_STATE = max(POOL_WINDOWS) - 1
N_GROUPS = 4
EXP_PER_GROUP = 4
N_EXPERTS = N_GROUPS * EXP_PER_GROUP
LN_EPS = 1e-5
ROUTE_ROWS = 32
VMEM_LIMIT = 56 * 2**20

_NT = (((1,), (1,)), ((), ()))


def _params(sem):
    return pltpu.CompilerParams(dimension_semantics=sem, vmem_limit_bytes=VMEM_LIMIT)


def _dot(a, b):
    return jnp.dot(a, b, preferred_element_type=F32)


def _layer_norm(v, g, b):
    mu = jnp.mean(v, axis=-1, keepdims=True)
    c = v - mu
    var = jnp.mean(c * c, axis=-1, keepdims=True)
    return c * lax.rsqrt(var + LN_EPS) * g + b


def _softplus(z):
    return jnp.maximum(z, 0.0) + jnp.log(1.0 + jnp.exp(jnp.minimum(z, -z)))


def _split_bf16(x):
    hi = x.astype(BF)
    lo = (x - hi.astype(F32)).astype(BF)
    return hi, lo


def _mm_kernel(x_ref, w_ref, *o_refs, offsets):
    acc = _dot(x_ref[...].astype(BF), w_ref[...])
    for o_ref, off in zip(o_refs, offsets):
        o_ref[...] = acc[:, off:off + o_ref.shape[-1]].astype(o_ref.dtype)


def _matmul(x, w, layer, outs, tm, name):
    m, k = x.shape
    n = w.shape[-1]
    return pl.pallas_call(
        functools.partial(_mm_kernel, offsets=tuple(o for o, _, _ in outs)),
        grid=(m // tm,),
        in_specs=[pl.BlockSpec((tm, k), lambda i: (i, 0)),
                  pl.BlockSpec((None, k, n), lambda i: (layer, 0, 0))],
        out_specs=[pl.BlockSpec((tm, c), lambda i: (i, 0)) for _, c, _ in outs],
        out_shape=[jax.ShapeDtypeStruct((m, c), dt) for _, c, dt in outs],
        compiler_params=_params(("parallel",)),
        name=name,
    )(x, w)


def _kv_kernel(x_ref, w_ref, kf_ref, vf_ref, kb_ref, vb_ref):
    n_heads = kf_ref.shape[1]
    main_w = n_heads * HEAD_DIM
    acc = _dot(x_ref[...], w_ref[...])
    for h in range(n_heads):
        kf_ref[0, h] = acc[:, h * HEAD_DIM:(h + 1) * HEAD_DIM]
        vf_ref[0, h] = acc[:, main_w + h * HEAD_DIM:main_w + (h + 1) * HEAD_DIM]
    kb_ref[...] = acc[:, :main_w].astype(BF)
    vb_ref[...] = acc[:, main_w:].astype(BF)


def _kv_project(xb, w_kv, b, t, tm):
    n, d = xb.shape
    main_w = w_kv.shape[1] // 2
    n_heads = main_w // HEAD_DIM
    nt = t // tm
    head_major = pl.BlockSpec((1, n_heads, tm, HEAD_DIM), lambda i: (i // nt, 0, i % nt, 0))
    return pl.pallas_call(
        _kv_kernel,
        grid=(n // tm,),
        in_specs=[pl.BlockSpec((tm, d), lambda i: (i, 0)),
                  pl.BlockSpec((d, 2 * main_w), lambda i: (0, 0))],
        out_specs=[head_major, head_major,
                   pl.BlockSpec((tm, main_w), lambda i: (i, 0)),
                   pl.BlockSpec((tm, main_w), lambda i: (i, 0))],
        out_shape=[jax.ShapeDtypeStruct((b, n_heads, t, HEAD_DIM), F32)] * 2
                  + [jax.ShapeDtypeStruct((n, main_w), BF)] * 2,
        compiler_params=_params(("parallel",)),
        name="proj_sb_kv",
    )(xb, w_kv)


def _memproj_kernel(x_ref, w_ref, o_ref):
    o_ref[0] = _dot(x_ref[...], w_ref[...])


def _mem_project(mem_bf, w_cat, width):
    m, d = mem_bf.shape
    nj = w_cat.shape[1] // width
    return pl.pallas_call(
        _memproj_kernel,
        grid=(nj,),
        in_specs=[pl.BlockSpec((m, d), lambda j: (0, 0)),
                  pl.BlockSpec((d, width), lambda j: (0, j))],
        out_specs=pl.BlockSpec((1, m, width), lambda j: (j, 0, 0)),
        out_shape=jax.ShapeDtypeStruct((nj, m, width), F32),
        compiler_params=_params(("parallel",)),
        name="mem_project",
    )(mem_bf, w_cat)


def _pool_kernel(u_ref, prev_ref, w_ref, sc_ref, y_ref, buf, *, tt, nt, pos0):
    t = pl.program_id(1)
    p = POOL_STATE
    cg = w_ref.shape[-1]

    @pl.when(t == 0)
    def _():
        buf[1:1 + p, :] = prev_ref[0]

    u = u_ref[0]
    buf[p + 1:p + 1 + tt, :] = u
    pos = lax.broadcasted_iota(I32, (tt, 1), 0) + (t * tt + pos0)
    for g, w in enumerate(POOL_WINDOWS):
        c0 = g * cg
        ug = u[:, c0:c0 + cg]
        s = ug
        for k in range(1, w):
            s = s + buf[p + 1 - k:p + 1 - k + tt, c0:c0 + cg]
        cnt = jnp.minimum(w, pos + 1).astype(F32)
        d = (s / cnt - ug).astype(BF)
        yg = _dot(d, w_ref[g]) * sc_ref[:, c0:c0 + cg]
        y_ref[0, :, c0:c0 + cg] = yg.astype(y_ref.dtype)
    if nt > 1:
        buf[1:1 + p, :] = buf[tt + 1:tt + 1 + p, :]


def _pool_mix(proj, prev, w_grp_bf, scale, main_w, tt, pos0):
    b, t, _ = proj.shape
    nt = t // tt
    cg = main_w // len(POOL_WINDOWS)
    return pl.pallas_call(
        functools.partial(_pool_kernel, tt=tt, nt=nt, pos0=pos0),
        grid=(b, nt),
        in_specs=[pl.BlockSpec((1, tt, main_w), lambda i, j: (i, j, 0)),
                  pl.BlockSpec((1, POOL_STATE, main_w), lambda i, j: (i, 0, 0)),
                  pl.BlockSpec((len(POOL_WINDOWS), cg, cg), lambda i, j: (0, 0, 0)),
                  pl.BlockSpec((1, main_w), lambda i, j: (0, 0))],
        out_specs=pl.BlockSpec((1, tt, main_w), lambda i, j: (i, j, 0)),
        out_shape=jax.ShapeDtypeStruct((b, t, main_w), BF),
        scratch_shapes=[pltpu.VMEM((POOL_STATE + 1 + tt, main_w), F32)],
        compiler_params=_params(("parallel", "arbitrary")),
        name="pool_mix",
    )(proj, prev, w_grp_bf, scale.reshape(1, main_w))


def _pool_dec_kernel(u_ref, prev_ref, w_ref, sc_ref, y_ref, *, pos0):
    p = POOL_STATE
    nt = u_ref.shape[0]
    cg = w_ref.shape[-1]
    cat = [prev_ref[k] for k in range(p)] + [u_ref[i] for i in range(nt)]
    for g, w in enumerate(POOL_WINDOWS):
        c0 = g * cg
        ds = []
        for i in range(nt):
            s = cat[p + i][:, c0:c0 + cg]
            for k in range(1, w):
                s = s + cat[p + i - k][:, c0:c0 + cg]
            cnt = float(min(w, pos0 + i + 1))
            ds.append(s / cnt - cat[p + i][:, c0:c0 + cg])
        d = jnp.concatenate(ds, axis=0).astype(BF)
        yg = _dot(d, w_ref[g]) * sc_ref[:, c0:c0 + cg]
        nb = ds[0].shape[0]
        for i in range(nt):
            y_ref[i, :, c0:c0 + cg] = yg[i * nb:(i + 1) * nb].astype(y_ref.dtype)


def _pool_mix_decode(u_tm, prev_tm, w_grp_bf, scale, pos0):
    nt, b, c = u_tm.shape
    return pl.pallas_call(
        functools.partial(_pool_dec_kernel, pos0=pos0),
        out_shape=jax.ShapeDtypeStruct((nt, b, c), BF),
        compiler_params=pltpu.CompilerParams(vmem_limit_bytes=VMEM_LIMIT),
        name="pool_mix_decode",
    )(u_tm, prev_tm, w_grp_bf, scale.reshape(1, c))


def _mem_kernel(q_ref, k_ref, v_ref, o_ref, *, cdt):
    nh = q_ref.shape[-1] // HEAD_DIM
    scale = HEAD_DIM ** -0.5
    for h in range(nh):
        sl = slice(h * HEAD_DIM, (h + 1) * HEAD_DIM)
        q = q_ref[0, :, sl].astype(cdt)
        k = k_ref[0, :, sl].astype(cdt)
        v = v_ref[0, :, sl].astype(cdt)
        s = lax.dot_general(q, k, _NT, preferred_element_type=F32) * scale
        e = jnp.exp(s - jnp.max(s, axis=-1, keepdims=True))
        den = jnp.sum(e, axis=-1, keepdims=True)
        o = _dot(e.astype(cdt), v) / den
        o_ref[0, :, sl] = o.astype(o_ref.dtype)


def _mem_attend(proj, mk, mv, mem_w, tt):
    b, t, mix_w = proj.shape
    n_mem = mk.shape[1]
    qblk = (mix_w - mem_w) // mem_w
    cdt = BF if tt % 16 == 0 else F32
    return pl.pallas_call(
        functools.partial(_mem_kernel, cdt=cdt),
        grid=(b, t // tt),
        in_specs=[pl.BlockSpec((1, tt, mem_w), lambda i, j: (i, j, qblk)),
                  pl.BlockSpec((1, n_mem, mem_w), lambda i, j: (i, 0, 0)),
                  pl.BlockSpec((1, n_mem, mem_w), lambda i, j: (i, 0, 0))],
        out_specs=pl.BlockSpec((1, tt, mem_w), lambda i, j: (i, j, 0)),
        out_shape=jax.ShapeDtypeStruct((b, t, mem_w), BF),
        compiler_params=_params(("parallel", "parallel")),
        name="mem_attend",
    )(proj, mk, mv)


def _sb_kernel(q_ref, k_ref, v_ref, bias_ref, o_ref, *, tq, hp):
    qi = pl.program_id(2)
    inv_sqrt = 1.0 / np.sqrt(np.float32(HEAD_DIM))
    jj = lax.broadcasted_iota(I32, (tq, tq), 0)
    ss = lax.broadcasted_iota(I32, (tq, tq), 1)
    later = (jj > ss).astype(BF)
    later2 = jnp.concatenate([later, later], axis=0)
    causal = ss < jj
    heads = [slice(h * HEAD_DIM, (h + 1) * HEAD_DIM) for h in range(hp)]
    qs = [(q_ref[0, :, sl] * inv_sqrt).astype(BF) for sl in heads]
    biases = [bias_ref[0, h:h + 1, :] for h in range(hp)]

    def blocks(start, state, mask):
        rng = range(hp)
        zs = [lax.dot_general(qs[h], k_ref[0, pl.ds(start, tq), heads[h]], _NT,
                              preferred_element_type=F32) + biases[h] for h in rng]
        sps = [_softplus(z) for z in zs]
        if mask is not None:
            sps = [jnp.where(mask, sp, 0.0) for sp in sps]
        splits = [_split_bf16(sp) for sp in sps]
        later_sums = [_dot(jnp.concatenate([hi, lo], axis=1), later2) for hi, lo in splits]
        ws = [jnp.exp(zs[h] - sps[h] - later_sums[h] + state[h][0]) for h in rng]
        if mask is not None:
            ws = [jnp.where(mask, a, 0.0) for a in ws]
        accs = [state[h][1] + _dot(ws[h].astype(BF), v_ref[0, pl.ds(start, tq), heads[h]])
                for h in rng]
        tails = [state[h][0] - jnp.sum(sps[h], axis=1, keepdims=True) for h in rng]
        return tuple(zip(tails, accs))

    diag = pl.multiple_of(qi * tq, tq)
    zero = (jnp.zeros((tq, 1), F32), jnp.zeros((tq, HEAD_DIM), F32))
    state = blocks(diag, (zero,) * hp, causal)

    def body(n, carry):
        return blocks(pl.multiple_of((qi - 1 - n) * tq, tq), carry, None)

    state = lax.fori_loop(0, qi, body, state)
    for h in range(hp):
        o_ref[0, :, heads[h]] = state[h][1].astype(o_ref.dtype)


def _stick_breaking_prompt(proj, k_bf, v_bf, bias, n_heads, tq, hp):
    b, t, _ = proj.shape
    main_w = n_heads * HEAD_DIM
    w = hp * HEAD_DIM
    bias_b = jnp.broadcast_to(bias.astype(F32).reshape(n_heads // hp, hp, 1),
                              (n_heads // hp, hp, tq))
    return pl.pallas_call(
        functools.partial(_sb_kernel, tq=tq, hp=hp),
        grid=(b, n_heads // hp, t // tq),
        in_specs=[pl.BlockSpec((1, tq, w), lambda i, h, j: (i, j, h)),
                  pl.BlockSpec((1, t, w), lambda i, h, j: (i, 0, h)),
                  pl.BlockSpec((1, t, w), lambda i, h, j: (i, 0, h)),
                  pl.BlockSpec((1, hp, tq), lambda i, h, j: (h, 0, 0))],
        out_specs=pl.BlockSpec((1, tq, w), lambda i, h, j: (i, j, h)),
        out_shape=jax.ShapeDtypeStruct((b, t, main_w), BF),
        compiler_params=_params(("parallel", "parallel", "arbitrary")),
        name="stick_breaking_prompt",
    )(proj, k_bf, v_bf, bias_b)


def _sb_dec_kernel(pt_ref, qt_ref, bias_ref, kn_ref, vn_ref, *rest, pages, n_q, n_new):
    k_refs = rest[:pages]
    v_refs = rest[pages:2 * pages]
    o_ref, tail_ref, acc_ref = rest[2 * pages:]
    j = pl.program_id(1)
    inv_sqrt = 1.0 / np.sqrt(np.float32(HEAD_DIM))
    qt = qt_ref[0]
    bias = bias_ref[...]
    n_cols = acc_ref.shape[0]
    nk = kn_ref.shape[1]
    n_heads = k_refs[0].shape[1]
    ss = lax.broadcasted_iota(I32, (nk, nk), 0)
    jj = lax.broadcasted_iota(I32, (nk, nk), 1)
    later = (jj > ss).astype(BF)

    def heads_cat(ref):
        return jnp.concatenate([ref[0, h] for h in range(n_heads)], axis=1).astype(BF)

    def scores(kcat, mask):
        z = _dot(kcat, qt) * inv_sqrt + bias
        sp = _softplus(z)
        if mask is not None:
            sp = jnp.where(mask, sp, 0.0)
        hi, lo = _split_bf16(sp)
        both = _dot(later, jnp.concatenate([hi, lo], axis=1))
        later_sum = both[:, :128] + both[:, 128:]
        return z, sp, later_sum, jnp.sum(sp, axis=0, keepdims=True)

    def weights(z, sp, later_sum, tail, mask):
        a = jnp.exp(z - sp - later_sum + tail)
        if mask is not None:
            a = jnp.where(mask, a, 0.0)
        return a.T[:n_cols].astype(BF)

    @pl.when(j == 0)
    def _():
        r = lax.broadcasted_iota(I32, (nk, 128), 0)
        c = lax.broadcasted_iota(I32, (nk, 128), 1)
        mask = (r < c % n_q) & (r < n_new)
        z, sp, later_sum, total = scores(kn_ref[0].astype(BF), mask)
        a_t = weights(z, sp, later_sum, jnp.zeros_like(total), mask)
        acc_ref[...] = _dot(a_t, vn_ref[0].astype(BF))
        tail_ref[...] = -total

    parts = [scores(heads_cat(k_refs[g]), None) for g in range(pages)]
    tail = tail_ref[...]
    a_ts = []
    for z, sp, later_sum, total in parts:
        a_ts.append(weights(z, sp, later_sum, tail, None))
        tail = tail - total
    v_all = jnp.concatenate([heads_cat(v_refs[g]) for g in range(pages)], axis=0)
    acc_ref[...] += _dot(jnp.concatenate(a_ts, axis=1), v_all)
    tail_ref[...] = tail

    @pl.when(j == pl.num_programs(1) - 1)
    def _():
        for h in range(o_ref.shape[-1] // HEAD_DIM):
            sl = slice(h * HEAD_DIM, (h + 1) * HEAD_DIM)
            o_ref[0, :, sl] = acc_ref[h * n_q:(h + 1) * n_q, sl].astype(o_ref.dtype)


def _stick_breaking_decode(q, k_new, v_new, cache_k, cache_v, page_table, bias, pages):
    b, t, main_w = q.shape
    n_pages = page_table.shape[1]
    _, page, n_heads, _ = cache_k.shape
    n_cols = n_heads * t
    qh = q.reshape(b, t, n_heads, HEAD_DIM).astype(F32)
    eye = jnp.eye(n_heads, dtype=F32)
    qt = jnp.einsum("bihd,hg->bhdgi", qh, eye).reshape(b, main_w, n_cols)
    qt = jnp.pad(qt, ((0, 0), (0, 0), (0, 128 - n_cols))).astype(BF)
    bias_c = jnp.pad(jnp.repeat(bias.astype(F32), t), (0, 128 - n_cols)).reshape(1, 128)
    new_rows = page
    k_pad = jnp.pad(k_new, ((0, 0), (0, new_rows - t), (0, 0)))
    v_pad = jnp.pad(v_new, ((0, 0), (0, new_rows - t), (0, 0)))

    def page_map(g):
        return lambda i, j, pt: (pt[i, n_pages - 1 - (j * pages + g)], 0, 0, 0)

    cache_k = jnp.transpose(cache_k, (0, 2, 1, 3))
    cache_v = jnp.transpose(cache_v, (0, 2, 1, 3))
    cache_specs = [pl.BlockSpec((1, n_heads, page, HEAD_DIM), page_map(g)) for g in range(pages)]
    grid_spec = pltpu.PrefetchScalarGridSpec(
        num_scalar_prefetch=1,
        grid=(b, n_pages // pages),
        in_specs=[pl.BlockSpec((1, main_w, 128), lambda i, j, pt: (i, 0, 0)),
                  pl.BlockSpec((1, 128), lambda i, j, pt: (0, 0)),
                  pl.BlockSpec((1, new_rows, main_w), lambda i, j, pt: (i, 0, 0)),
                  pl.BlockSpec((1, new_rows, main_w), lambda i, j, pt: (i, 0, 0))]
                 + cache_specs + cache_specs,
        out_specs=pl.BlockSpec((1, t, main_w), lambda i, j, pt: (i, 0, 0)),
        scratch_shapes=[pltpu.VMEM((1, 128), F32), pltpu.VMEM((n_cols, main_w), F32)],
    )
    return pl.pallas_call(
        functools.partial(_sb_dec_kernel, pages=pages, n_q=t, n_new=t),
        grid_spec=grid_spec,
        out_shape=jax.ShapeDtypeStruct((b, t, main_w), BF),
        compiler_params=_params(("parallel", "arbitrary")),
        name="stick_breaking_decode",
    )(page_table, qt, bias_c, k_pad, v_pad, *([cache_k] * pages), *([cache_v] * pages))


def _out_ln_kernel(ym_ref, ye_ref, x_ref, wa_ref, wb_ref, g_ref, b_ref, rh_ref, rl_ref,
                   x1_ref, lt_ref, *, alpha):
    h = _dot(ym_ref[...], wa_ref[...]) + _dot(ye_ref[...], wb_ref[...])
    x1 = _layer_norm(alpha * x_ref[...] + h, g_ref[...], b_ref[...])
    x1_ref[...] = x1
    xh, xl = _split_bf16(x1)
    rh = rh_ref[...]
    lt = (lax.dot_general(rh, xh, _NT, preferred_element_type=F32)
          + lax.dot_general(rh, xl, _NT, preferred_element_type=F32)
          + lax.dot_general(rl_ref[...], xh, _NT, preferred_element_type=F32))
    lt_ref[...] = lt


def _out_ln_route(y_main, y_mem, x, w_out_bf, layer, gamma, beta, r_hi, r_lo, alpha, tm):
    n, d = x.shape
    main_w = y_main.shape[1]
    mem_w = y_mem.shape[1]
    return pl.pallas_call(
        functools.partial(_out_ln_kernel, alpha=alpha),
        grid=(n // tm,),
        in_specs=[pl.BlockSpec((tm, main_w), lambda i: (i, 0)),
                  pl.BlockSpec((tm, mem_w), lambda i: (i, 0)),
                  pl.BlockSpec((tm, d), lambda i: (i, 0)),
                  pl.BlockSpec((None, main_w, d), lambda i: (layer, 0, 0)),
                  pl.BlockSpec((None, mem_w, d), lambda i: (layer, main_w // mem_w, 0)),
                  pl.BlockSpec((1, d), lambda i: (0, 0)),
                  pl.BlockSpec((1, d), lambda i: (0, 0)),
                  pl.BlockSpec((ROUTE_ROWS, d), lambda i: (0, 0)),
                  pl.BlockSpec((ROUTE_ROWS, d), lambda i: (0, 0))],
        out_specs=[pl.BlockSpec((tm, d), lambda i: (i, 0)),
                   pl.BlockSpec((ROUTE_ROWS, tm), lambda i: (0, i))],
        out_shape=[jax.ShapeDtypeStruct((n, d), F32),
                   jax.ShapeDtypeStruct((ROUTE_ROWS, n), F32)],
        compiler_params=_params(("parallel",)),
        name="out_ln_route",
    )(y_main, y_mem, x, w_out_bf, w_out_bf, gamma.reshape(1, d), beta.reshape(1, d), r_hi, r_lo)


def _route_kernel(lt_ref, rb_ref, e1_ref, e2_ref, g1_ref, g2_ref):
    lt = lt_ref[...] + rb_ref[...]
    ng, ne = N_GROUPS, EXP_PER_GROUP
    lg = [lt[g:g + 1, :] for g in range(ng)]
    m = lg[0]
    gi = jnp.zeros(m.shape, I32)
    for g in range(1, ng):
        better = lg[g] > m
        gi = jnp.where(better, g, gi)
        m = jnp.where(better, lg[g], m)
    den = jnp.exp(lg[0] - m)
    for g in range(1, ng):
        den = den + jnp.exp(lg[g] - m)
    p_sel = 1.0 / den
    le = []
    for e in range(ne):
        v = lt[ng + e:ng + e + 1, :]
        for g in range(1, ng):
            r = ng + g * ne + e
            v = jnp.where(gi == g, lt[r:r + 1, :], v)
        le.append(v)
    v1 = le[0]
    i1 = jnp.zeros(m.shape, I32)
    for e in range(1, ne):
        better = le[e] > v1
        i1 = jnp.where(better, e, i1)
        v1 = jnp.where(better, le[e], v1)
    v2 = jnp.full(m.shape, -jnp.inf, F32)
    i2 = jnp.zeros(m.shape, I32)
    for e in range(ne):
        better = (i1 != e) & (le[e] > v2)
        i2 = jnp.where(better, e, i2)
        v2 = jnp.where(better, le[e], v2)
    t = jnp.exp(v2 - v1)
    w_a = 1.0 / (1.0 + t)
    w_b = t / (1.0 + t)
    e1_ref[...] = gi * ne + i1
    e2_ref[...] = gi * ne + i2
    g1_ref[...] = p_sel * w_a
    g2_ref[...] = p_sel * w_b


def _route(lt, rbias):
    n = lt.shape[1]
    shp = lambda dt: jax.ShapeDtypeStruct((1, n), dt)
    return pl.pallas_call(
        _route_kernel,
        out_shape=[shp(I32), shp(I32), shp(F32), shp(F32)],
        compiler_params=pltpu.CompilerParams(vmem_limit_bytes=VMEM_LIMIT),
        name="route",
    )(lt, rbias)


def _moe_plan(e1, e2, tm, n_tiles):
    n = e1.shape[0]
    e = jnp.concatenate([e1, e2])
    oh = (e[:, None] == jnp.arange(N_EXPERTS, dtype=I32)[None, :]).astype(I32)
    csum = jnp.cumsum(oh, axis=0)
    rank = jnp.sum((csum - oh) * oh, axis=1)
    counts = csum[-1]
    tiles = (counts + tm - 1) // tm
    tile_end = jnp.cumsum(tiles)
    starts = (tile_end - tiles) * tm
    dest = jnp.sum(oh * starts[None, :], axis=1) + rank
    n_used = tile_end[-1]
    tidx = jnp.arange(n_tiles, dtype=I32)
    te = jnp.sum((tidx[:, None] >= tile_end[None, :]).astype(I32), axis=1)
    last_e = jnp.max(jnp.where(tiles > 0, jnp.arange(N_EXPERTS, dtype=I32), 0))
    te = jnp.where(tidx < n_used, te, last_e).astype(I32)
    rows = n_tiles * tm
    pair_of_row = jnp.full((rows,), -1, I32).at[dest].set(jnp.arange(2 * n, dtype=I32))
    has_pair = pair_of_row >= 0
    filler = jnp.logical_and(~has_pair, jnp.arange(rows, dtype=I32) < n_used * tm)
    spill = 2 * n + tm + jnp.cumsum(filler.astype(I32)) - 1
    src = jnp.where(has_pair, pair_of_row % n, 0).astype(I32)
    dst_rows = jnp.where(has_pair, pair_of_row, jnp.where(filler, spill, 2 * n))
    dst = jnp.concatenate([2 * n + jnp.arange(tm, dtype=I32), dst_rows]).astype(I32)
    return src, dst, te, n_used.reshape(1).astype(I32)


def _moe_kernel(te_ref, src_ref, dst_ref, nu_ref, x_hbm, wg_ref, wu_ref, wd_ref, y_hbm,
                xbuf, ybuf, gsem, ssem, wg_s, wu_s, wd_s, *, tm, n_pairs, n_spill_tiles):
    j = pl.program_id(0)
    n_used = nu_ref[0]
    slot = j % 2
    other = 1 - slot

    def start_gather(tile, to_slot):
        for r in range(tm):
            row = src_ref[tile * tm + r]
            pltpu.make_async_copy(x_hbm.at[pl.ds(row, 1)], xbuf.at[to_slot, pl.ds(r, 1)],
                                  gsem.at[to_slot]).start()

    def wait_gather(of_slot):
        pltpu.make_async_copy(x_hbm.at[pl.ds(0, tm)], xbuf.at[of_slot], gsem.at[of_slot]).wait()

    def start_scatter(tile, from_slot):
        for r in range(tm):
            row = dst_ref[(tile + 1) * tm + r]
            pltpu.make_async_copy(ybuf.at[from_slot, pl.ds(r, 1)], y_hbm.at[pl.ds(row, 1)],
                                  ssem.at[from_slot]).start()

    def wait_scatter(of_slot):
        pltpu.make_async_copy(ybuf.at[of_slot], y_hbm.at[pl.ds(0, tm)], ssem.at[of_slot]).wait()

    @pl.when(j == 0)
    def _():
        start_gather(0, 0)
        ybuf[1] = jnp.zeros(ybuf.shape[1:], F32)
        fills = [pltpu.make_async_copy(ybuf.at[1], y_hbm.at[pl.ds(n_pairs + c * tm, tm)],
                                       ssem.at[1]) for c in range(1, n_spill_tiles)]
        for f in fills:
            f.start()
        for f in fills:
            f.wait()

    e = te_ref[j]
    e_prev = te_ref[jnp.maximum(j - 1, 0)]

    @pl.when((j == 0) | (e != e_prev))
    def _():
        wg_s[...] = wg_ref[0, 0].astype(BF)
        wu_s[...] = wu_ref[0, 0].astype(BF)
        wd_s[...] = wd_ref[0, 0].astype(BF)

    @pl.when((j >= 1) & (j <= n_used))
    def _():
        wait_scatter(slot)

    @pl.when(j < n_used)
    def _():
        wait_gather(slot)
        start_gather(j + 1, other)
        start_scatter(j - 1, other)
        x = xbuf[slot].astype(BF)
        g = _dot(x, wg_s[...])
        u = _dot(x, wu_s[...])
        h = (g * jax.nn.sigmoid(g) * u).astype(BF)
        ybuf[slot] = _dot(h, wd_s[...])

    @pl.when(j == n_used)
    def _():
        wait_gather(slot)
        start_scatter(j - 1, other)
        wait_scatter(other)


def _moe_experts(x1, src, dst, te, n_used, w_gate, w_up, w_down, layer, tm, n_tiles):
    n, d = x1.shape
    ff = w_gate.shape[-1]
    n_spill_tiles = 1 + -(-N_EXPERTS * (tm - 1) // tm)
    grid_spec = pltpu.PrefetchScalarGridSpec(
        num_scalar_prefetch=4,
        grid=(n_tiles,),
        in_specs=[pl.BlockSpec(memory_space=pl.ANY),
                  pl.BlockSpec((1, 1, d, ff), lambda j, te, *_: (layer, te[j], 0, 0)),
                  pl.BlockSpec((1, 1, d, ff), lambda j, te, *_: (layer, te[j], 0, 0)),
                  pl.BlockSpec((1, 1, ff, d), lambda j, te, *_: (layer, te[j], 0, 0))],
        out_specs=pl.BlockSpec(memory_space=pl.ANY),
        scratch_shapes=[pltpu.VMEM((2, tm, d), F32),
                        pltpu.VMEM((2, tm, d), F32),
                        pltpu.SemaphoreType.DMA((2,)),
                        pltpu.SemaphoreType.DMA((2,)),
                        pltpu.VMEM((d, ff), BF),
                        pltpu.VMEM((d, ff), BF),
                        pltpu.VMEM((ff, d), BF)],
    )
    return pl.pallas_call(
        functools.partial(_moe_kernel, tm=tm, n_pairs=2 * n, n_spill_tiles=n_spill_tiles),
        grid_spec=grid_spec,
        out_shape=jax.ShapeDtypeStruct((2 * n + n_spill_tiles * tm, d), F32),
        compiler_params=_params(("arbitrary",)),
        name="moe_experts",
    )(te, src, dst, n_used, x1, w_gate, w_up, w_down)


def _combine_kernel(x1_ref, ya_ref, yb_ref, g1_ref, g2_ref, gam_ref, bet_ref,
                    x2_ref, x2b_ref, *, alpha):
    m = g1_ref[...] * ya_ref[...] + g2_ref[...] * yb_ref[...]
    x2 = _layer_norm(alpha * x1_ref[...] + m, gam_ref[...], bet_ref[...])
    x2_ref[...] = x2
    x2b_ref[...] = x2.astype(BF)


def _combine_ln(x1, y_pairs, g1, g2, gamma, beta, alpha, tm):
    n, d = x1.shape
    nb = n // tm
    return pl.pallas_call(
        functools.partial(_combine_kernel, alpha=alpha),
        grid=(nb,),
        in_specs=[pl.BlockSpec((tm, d), lambda i: (i, 0)),
                  pl.BlockSpec((tm, d), lambda i: (i, 0)),
                  pl.BlockSpec((tm, d), lambda i: (i + nb, 0)),
                  pl.BlockSpec((tm, 1), lambda i: (i, 0)),
                  pl.BlockSpec((tm, 1), lambda i: (i, 0)),
                  pl.BlockSpec((1, d), lambda i: (0, 0)),
                  pl.BlockSpec((1, d), lambda i: (0, 0))],
        out_specs=[pl.BlockSpec((tm, d), lambda i: (i, 0)),
                   pl.BlockSpec((tm, d), lambda i: (i, 0))],
        out_shape=[jax.ShapeDtypeStruct((n, d), F32), jax.ShapeDtypeStruct((n, d), BF)],
        compiler_params=_params(("parallel",)),
        name="combine_ln",
    )(x1, y_pairs, y_pairs, g1.reshape(n, 1), g2.reshape(n, 1), gamma.reshape(1, d),
      beta.reshape(1, d))


def _hier_moe_ln(x1, lt, rbias, w_gate, w_up, w_down, layer, gamma, beta, alpha, tm_moe, tm_tok):
    n = x1.shape[0]
    e1, e2, g1, g2 = _route(lt, rbias)
    n_tiles = (2 * n + N_EXPERTS * (tm_moe - 1)) // tm_moe + 1
    src, dst, te, n_used = _moe_plan(e1[0], e2[0], tm_moe, n_tiles)
    y_pairs = _moe_experts(x1, src, dst, te, n_used, w_gate, w_up, w_down, layer, tm_moe, n_tiles)
    return _combine_ln(x1, y_pairs, g1[0], g2[0], gamma, beta, alpha, tm_tok)


def _trunk(x3, pool_prev, mem_k, mem_v, sb_past, p, *, decode):
    b, t, d = x3.shape
    depth = p["w_in"].shape[0]
    n_a = p["w_pool"].shape[0]
    main_w = p["w_sb_kv"].shape[-1] // 2
    mem_w = d - main_w
    n_heads = main_w // HEAD_DIM
    alpha = float((2 * depth) ** 0.25)
    n = b * t
    tm = min(512, n)
    tm_c = min(256, n)
    tm_moe = 256 if n >= 2048 else 16
    pos0 = 0 if not decode else sb_past[2].shape[1] * sb_past[0].shape[1]

    x = x3.reshape(n, d)
    xb = x
    new_pool = []
    k_bf = v_bf = k_f = v_f = None
    for l in range(depth):
        (proj,) = _matmul(xb, p["w_in"], l, [(0, d, F32)], tm, "proj_in")
        proj3 = proj.reshape(b, t, d)
        main = proj3[..., :main_w]
        if l < n_a:
            if decode:
                u_tm = jnp.transpose(main, (1, 0, 2))
                prev_tm = jnp.transpose(pool_prev[l], (1, 0, 2))
                y_tm = _pool_mix_decode(u_tm, prev_tm, p["w_pool"][l], p["pool_scale"][l], pos0)
                y_main = jnp.transpose(y_tm, (1, 0, 2))
            else:
                y_main = _pool_mix(proj3, pool_prev[l], p["w_pool"][l], p["pool_scale"][l],
                                   main_w, min(256, t), pos0)
            if t >= POOL_STATE:
                new_pool.append(main[:, t - POOL_STATE:, :])
            else:
                new_pool.append(jnp.concatenate([pool_prev[l][:, t:, :], main], axis=1))
        else:
            bias = p["sb_bias"][l - n_a]
            if decode:
                y_main = _stick_breaking_decode(main, k_f.reshape(b, t, main_w),
                                                v_f.reshape(b, t, main_w),
                                                sb_past[0], sb_past[1], sb_past[2], bias, pages=4)
            else:
                y_main = _stick_breaking_prompt(proj3, k_bf.reshape(b, t, main_w),
                                                v_bf.reshape(b, t, main_w), bias, n_heads, 256, 4)
        if decode:
            t_pad = 8
            proj_pad = jnp.pad(proj3, ((0, 0), (0, t_pad - t), (0, 0)))
            y_mem = _mem_attend(proj_pad, mem_k[l], mem_v[l], mem_w, t_pad)[:, :t]
        else:
            y_mem = _mem_attend(proj3, mem_k[l], mem_v[l], mem_w, min(512, t))
        x1, lt = _out_ln_route(y_main.reshape(n, main_w), y_mem.reshape(n, mem_w), x,
                               p["w_out"], l, p["ln1_g"][l], p["ln1_b"][l],
                               p["r_hi"][l], p["r_lo"][l], alpha, tm_c)
        x, xb = _hier_moe_ln(x1, lt, p["r_bias"][l], p["w_gate"], p["w_up"], p["w_down"], l,
                             p["ln2_g"][l], p["ln2_b"][l], alpha, tm_moe, tm)
        if l == n_a - 1:
            if decode:
                k_f, v_f = _matmul(xb, p["w_sb_kv"], 0, [(0, main_w, F32), (main_w, main_w, F32)],
                                   n, "proj_sb_kv")
                k_out = k_f.reshape(b, t, n_heads, HEAD_DIM)
                v_out = v_f.reshape(b, t, n_heads, HEAD_DIM)
            else:
                k_hm, v_hm, k_bf, v_bf = _kv_project(xb, p["w_sb_kv"][0], b, t, 256)
                k_out = jnp.transpose(k_hm, (0, 2, 1, 3))
                v_out = jnp.transpose(v_hm, (0, 2, 1, 3))
    return x.reshape(b, t, d), jnp.stack(new_pool, axis=0), k_out, v_out


def kernel(x_prompt, x_sample, state_pool, cache_sb_k, cache_sb_v, cache_mem_k, cache_mem_v,
           page_table, mem_prompt, w_in, w_out, w_pool_grp, pool_scale, w_mem_k, w_mem_v,
           ln1_g, ln1_b, ln2_g, ln2_b, w_route_grp, b_route_grp, w_route_exp, b_route_exp,
           w_gate, w_up, w_down, w_sb_k, w_sb_v, sb_bias):
    depth, d, _ = w_in.shape
    bp, _, _ = x_prompt.shape
    bs, ts, _ = x_sample.shape
    n_a = w_pool_grp.shape[0]
    main_w = w_sb_k.shape[1]
    mem_w = w_mem_k.shape[2]
    n_mem = mem_prompt.shape[1]
    n_heads = main_w // HEAD_DIM
    mem_heads = mem_w // HEAD_DIM

    wr = jnp.concatenate([w_route_grp, w_route_exp.reshape(depth, d, N_EXPERTS)], axis=-1)
    wr_t = jnp.pad(jnp.transpose(wr, (0, 2, 1)),
                   ((0, 0), (0, ROUTE_ROWS - N_GROUPS - N_EXPERTS), (0, 0)))
    r_hi = wr_t.astype(BF)
    r_lo = (wr_t - r_hi.astype(F32)).astype(BF)
    r_bias = jnp.pad(jnp.concatenate([b_route_grp, b_route_exp.reshape(depth, N_EXPERTS)], axis=-1),
                     ((0, 0), (0, ROUTE_ROWS - N_GROUPS - N_EXPERTS))).reshape(depth, ROUTE_ROWS, 1)
    p = dict(
        w_in=w_in.astype(BF), w_out=w_out.astype(BF), w_pool=w_pool_grp.astype(BF),
        pool_scale=pool_scale, w_sb_kv=jnp.concatenate([w_sb_k, w_sb_v], axis=1).astype(BF)[None],
        ln1_g=ln1_g, ln1_b=ln1_b, ln2_g=ln2_g, ln2_b=ln2_b, r_hi=r_hi, r_lo=r_lo, r_bias=r_bias,
        w_gate=w_gate, w_up=w_up, w_down=w_down, sb_bias=sb_bias)

    w_mem = jnp.concatenate([jnp.transpose(w_mem_k, (1, 0, 2)).reshape(d, depth * mem_w),
                             jnp.transpose(w_mem_v, (1, 0, 2)).reshape(d, depth * mem_w)],
                            axis=1).astype(BF)
    mem_kv = _mem_project(mem_prompt.reshape(bp * n_mem, d).astype(BF), w_mem, mem_w)
    mem_kv = mem_kv.reshape(2, depth, bp, n_mem, mem_w)
    pool0 = jnp.zeros((n_a, bp, POOL_STATE, main_w), x_prompt.dtype)
    y_p, pool_p, k_p, v_p = _trunk(x_prompt, pool0, mem_kv[0], mem_kv[1], None, p, decode=False)

    sb_past = (cache_sb_k, cache_sb_v, page_table)
    y_s, pool_s, k_s, v_s = _trunk(
        x_sample, state_pool, cache_mem_k.reshape(depth, bs, n_mem, mem_w),
        cache_mem_v.reshape(depth, bs, n_mem, mem_w), sb_past, p, decode=True)

    return (y_p, y_s, pool_p, pool_s, k_p, v_p, k_s, v_s,
            mem_kv[0].reshape(depth, bp, n_mem, mem_heads, HEAD_DIM),
            mem_kv[1].reshape(depth, bp, n_mem, mem_heads, HEAD_DIM))
```

```python
import functools

import jax
import jax.numpy as jnp
import numpy as np
from jax import lax
from jax.experimental import pallas as pl
from jax.experimental.pallas import tpu as pltpu

F32 = jnp.float32
BF = jnp.bfloat16
I32 = jnp.int32

HEAD_DIM = 128
LANES = 128
POOL_WINDOWS = (2, 4, 8, 16)
POOL_STATE = max(POOL_WINDOWS) - 1
N_GROUPS = 4
EXP_PER_GROUP = 4
N_EXPERTS = N_GROUPS * EXP_PER_GROUP
LN_EPS = 1e-5
ROUTE_ROWS = 32
VMEM_LIMIT = 56 * 2**20

_NT = (((1,), (1,)), ((), ()))


def _params(sem):
    return pltpu.CompilerParams(dimension_semantics=sem, vmem_limit_bytes=VMEM_LIMIT)


def _dot(a, b):
    return jnp.dot(a, b, preferred_element_type=F32)


def _layer_norm(v, g, b):
    mu = jnp.mean(v, axis=-1, keepdims=True)
    c = v - mu
    var = jnp.mean(c * c, axis=-1, keepdims=True)
    return c * lax.rsqrt(var + LN_EPS) * g + b


def _softplus(z):
    return jnp.maximum(z, 0.0) + jnp.log(1.0 + jnp.exp(jnp.minimum(z, -z)))


def _split_bf16(x):
    hi = x.astype(BF)
    lo = (x - hi.astype(F32)).astype(BF)
    return hi, lo


def _mm_kernel(x_ref, w_ref, *o_refs, offsets):
    acc = _dot(x_ref[...].astype(BF), w_ref[...])
    for o_ref, off in zip(o_refs, offsets):
        o_ref[...] = acc[:, off:off + o_ref.shape[-1]].astype(o_ref.dtype)


def _matmul(x, w, layer, outs, tm, name):
    m, k = x.shape
    n = w.shape[-1]
    return pl.pallas_call(
        functools.partial(_mm_kernel, offsets=tuple(o for o, _, _ in outs)),
        grid=(m // tm,),
        in_specs=[pl.BlockSpec((tm, k), lambda i: (i, 0)),
                  pl.BlockSpec((None, k, n), lambda i: (layer, 0, 0))],
        out_specs=[pl.BlockSpec((tm, c), lambda i: (i, 0)) for _, c, _ in outs],
        out_shape=[jax.ShapeDtypeStruct((m, c), dt) for _, c, dt in outs],
        compiler_params=_params(("parallel",)),
        name=name,
    )(x, w)


def _kv_kernel(x_ref, w_ref, kf_ref, vf_ref, kb_ref, vb_ref):
    n_heads = kf_ref.shape[1]
    main_w = n_heads * HEAD_DIM
    acc = _dot(x_ref[...], w_ref[...])
    for h in range(n_heads):
        kf_ref[0, h] = acc[:, h * HEAD_DIM:(h + 1) * HEAD_DIM]
        vf_ref[0, h] = acc[:, main_w + h * HEAD_DIM:main_w + (h + 1) * HEAD_DIM]
    kb_ref[...] = acc[:, :main_w].astype(BF)
    vb_ref[...] = acc[:, main_w:].astype(BF)


def _kv_project(xb, w_kv, b, t, tm):
    n, d = xb.shape
    main_w = w_kv.shape[1] // 2
    n_heads = main_w // HEAD_DIM
    nt = t // tm
    head_major = pl.BlockSpec((1, n_heads, tm, HEAD_DIM), lambda i: (i // nt, 0, i % nt, 0))
    return pl.pallas_call(
        _kv_kernel,
        grid=(n // tm,),
        in_specs=[pl.BlockSpec((tm, d), lambda i: (i, 0)),
                  pl.BlockSpec((d, 2 * main_w), lambda i: (0, 0))],
        out_specs=[head_major, head_major,
                   pl.BlockSpec((tm, main_w), lambda i: (i, 0)),
                   pl.BlockSpec((tm, main_w), lambda i: (i, 0))],
        out_shape=[jax.ShapeDtypeStruct((b, n_heads, t, HEAD_DIM), F32)] * 2
                  + [jax.ShapeDtypeStruct((n, main_w), BF)] * 2,
        compiler_params=_params(("parallel",)),
        name="proj_sb_kv",
    )(xb, w_kv)


def _memproj_kernel(x_ref, w_ref, o_ref):
    o_ref[0] = _dot(x_ref[...], w_ref[...])


def _mem_project(mem_bf, w_cat, width):
    m, d = mem_bf.shape
    nj = w_cat.shape[1] // width
    return pl.pallas_call(
        _memproj_kernel,
        grid=(nj,),
        in_specs=[pl.BlockSpec((m, d), lambda j: (0, 0)),
                  pl.BlockSpec((d, width), lambda j: (0, j))],
        out_specs=pl.BlockSpec((1, m, width), lambda j: (j, 0, 0)),
        out_shape=jax.ShapeDtypeStruct((nj, m, width), F32),
        compiler_params=_params(("parallel",)),
        name="mem_project",
    )(mem_bf, w_cat)


def _pool_kernel(u_ref, prev_ref, w_ref, sc_ref, y_ref, buf, *, tt, nt, pos0):
    t = pl.program_id(1)
    p = POOL_STATE
    cg = w_ref.shape[-1]

    @pl.when(t == 0)
    def _():
        buf[1:1 + p, :] = prev_ref[0]

    u = u_ref[0]
    buf[p + 1:p + 1 + tt, :] = u
    pos = lax.broadcasted_iota(I32, (tt, 1), 0) + (t * tt + pos0)
    for g, w in enumerate(POOL_WINDOWS):
        c0 = g * cg
        ug = u[:, c0:c0 + cg]
        s = ug
        for k in range(1, w):
            s = s + buf[p + 1 - k:p + 1 - k + tt, c0:c0 + cg]
        cnt = jnp.minimum(w, pos + 1).astype(F32)
        d = (s / cnt - ug).astype(BF)
        yg = _dot(d, w_ref[g]) * sc_ref[:, c0:c0 + cg]
        y_ref[0, :, c0:c0 + cg] = yg.astype(y_ref.dtype)
    if nt > 1:
        buf[1:1 + p, :] = buf[tt + 1:tt + 1 + p, :]


def _pool_mix(proj, prev, w_grp_bf, scale, main_w, tt, pos0):
    b, t, _ = proj.shape
    nt = t // tt
    cg = main_w // len(POOL_WINDOWS)
    return pl.pallas_call(
        functools.partial(_pool_kernel, tt=tt, nt=nt, pos0=pos0),
        grid=(b, nt),
        in_specs=[pl.BlockSpec((1, tt, main_w), lambda i, j: (i, j, 0)),
                  pl.BlockSpec((1, POOL_STATE, main_w), lambda i, j: (i, 0, 0)),
                  pl.BlockSpec((len(POOL_WINDOWS), cg, cg), lambda i, j: (0, 0, 0)),
                  pl.BlockSpec((1, main_w), lambda i, j: (0, 0))],
        out_specs=pl.BlockSpec((1, tt, main_w), lambda i, j: (i, j, 0)),
        out_shape=jax.ShapeDtypeStruct((b, t, main_w), BF),
        scratch_shapes=[pltpu.VMEM((POOL_STATE + 1 + tt, main_w), F32)],
        compiler_params=_params(("parallel", "arbitrary")),
        name="pool_mix",
    )(proj, prev, w_grp_bf, scale.reshape(1, main_w))


def _pool_dec_kernel(u_ref, prev_ref, w_ref, sc_ref, y_ref, *, pos0):
    p = POOL_STATE
    nt = u_ref.shape[0]
    cg = w_ref.shape[-1]
    cat = [prev_ref[k] for k in range(p)] + [u_ref[i] for i in range(nt)]
    for g, w in enumerate(POOL_WINDOWS):
        c0 = g * cg
        ds = []
        for i in range(nt):
            s = cat[p + i][:, c0:c0 + cg]
            for k in range(1, w):
                s = s + cat[p + i - k][:, c0:c0 + cg]
            cnt = float(min(w, pos0 + i + 1))
            ds.append(s / cnt - cat[p + i][:, c0:c0 + cg])
        d = jnp.concatenate(ds, axis=0).astype(BF)
        yg = _dot(d, w_ref[g]) * sc_ref[:, c0:c0 + cg]
        nb = ds[0].shape[0]
        for i in range(nt):
            y_ref[i, :, c0:c0 + cg] = yg[i * nb:(i + 1) * nb].astype(y_ref.dtype)


def _pool_mix_decode(u_tm, prev_tm, w_grp_bf, scale, pos0):
    nt, b, c = u_tm.shape
    return pl.pallas_call(
        functools.partial(_pool_dec_kernel, pos0=pos0),
        out_shape=jax.ShapeDtypeStruct((nt, b, c), BF),
        compiler_params=pltpu.CompilerParams(vmem_limit_bytes=VMEM_LIMIT),
        name="pool_mix_decode",
    )(u_tm, prev_tm, w_grp_bf, scale.reshape(1, c))


def _mem_kernel(q_ref, k_ref, v_ref, o_ref, *, cdt):
    nh = q_ref.shape[-1] // HEAD_DIM
    scale = HEAD_DIM ** -0.5
    for h in range(nh):
        sl = slice(h * HEAD_DIM, (h + 1) * HEAD_DIM)
        q = q_ref[0, :, sl].astype(cdt)
        k = k_ref[0, :, sl].astype(cdt)
        v = v_ref[0, :, sl].astype(cdt)
        s = lax.dot_general(q, k, _NT, preferred_element_type=F32) * scale
        e = jnp.exp(s - jnp.max(s, axis=-1, keepdims=True))
        den = jnp.sum(e, axis=-1, keepdims=True)
        o = _dot(e.astype(cdt), v) / den
        o_ref[0, :, sl] = o.astype(o_ref.dtype)


def _mem_attend(proj, mk, mv, mem_w, tt):
    b, t, mix_w = proj.shape
    n_mem = mk.shape[1]
    qblk = (mix_w - mem_w) // mem_w
    cdt = BF if tt % 16 == 0 else F32
    return pl.pallas_call(
        functools.partial(_mem_kernel, cdt=cdt),
        grid=(b, t // tt),
        in_specs=[pl.BlockSpec((1, tt, mem_w), lambda i, j: (i, j, qblk)),
                  pl.BlockSpec((1, n_mem, mem_w), lambda i, j: (i, 0, 0)),
                  pl.BlockSpec((1, n_mem, mem_w), lambda i, j: (i, 0, 0))],
        out_specs=pl.BlockSpec((1, tt, mem_w), lambda i, j: (i, j, 0)),
        out_shape=jax.ShapeDtypeStruct((b, t, mem_w), BF),
        compiler_params=_params(("parallel", "parallel")),
        name="mem_attend",
    )(proj, mk, mv)


def _sb_kernel(q_ref, k_ref, v_ref, bias_ref, o_ref, *, tq, hp):
    qi = pl.program_id(2)
    inv_sqrt = 1.0 / np.sqrt(np.float32(HEAD_DIM))
    jj = lax.broadcasted_iota(I32, (tq, tq), 0)
    ss = lax.broadcasted_iota(I32, (tq, tq), 1)
    later = (jj > ss).astype(BF)
    later2 = jnp.concatenate([later, later], axis=0)
    causal = ss < jj
    heads = [slice(h * HEAD_DIM, (h + 1) * HEAD_DIM) for h in range(hp)]
    qs = [(q_ref[0, :, sl] * inv_sqrt).astype(BF) for sl in heads]
    biases = [bias_ref[0, h:h + 1, :] for h in range(hp)]

    def blocks(start, state, mask):
        rng = range(hp)
        zs = [lax.dot_general(qs[h], k_ref[0, pl.ds(start, tq), heads[h]], _NT,
                              preferred_element_type=F32) + biases[h] for h in rng]
        sps = [_softplus(z) for z in zs]
        if mask is not None:
            sps = [jnp.where(mask, sp, 0.0) for sp in sps]
        splits = [_split_bf16(sp) for sp in sps]
        later_sums = [_dot(jnp.concatenate([hi, lo], axis=1), later2) for hi, lo in splits]
        ws = [jnp.exp(zs[h] - sps[h] - later_sums[h] + state[h][0]) for h in rng]
        if mask is not None:
            ws = [jnp.where(mask, a, 0.0) for a in ws]
        accs = [state[h][1] + _dot(ws[h].astype(BF), v_ref[0, pl.ds(start, tq), heads[h]])
                for h in rng]
        tails = [state[h][0] - jnp.sum(sps[h], axis=1, keepdims=True) for h in rng]
        return tuple(zip(tails, accs))

    diag = pl.multiple_of(qi * tq, tq)
    zero = (jnp.zeros((tq, 1), F32), jnp.zeros((tq, HEAD_DIM), F32))
    state = blocks(diag, (zero,) * hp, causal)

    def body(n, carry):
        return blocks(pl.multiple_of((qi - 1 - n) * tq, tq), carry, None)

    state = lax.fori_loop(0, qi, body, state)
    for h in range(hp):
        o_ref[0, :, heads[h]] = state[h][1].astype(o_ref.dtype)


def _stick_breaking_prompt(proj, k_bf, v_bf, bias, n_heads, tq, hp):
    b, t, _ = proj.shape
    main_w = n_heads * HEAD_DIM
    w = hp * HEAD_DIM
    bias_b = jnp.broadcast_to(bias.astype(F32).reshape(n_heads // hp, hp, 1),
                              (n_heads // hp, hp, tq))
    return pl.pallas_call(
        functools.partial(_sb_kernel, tq=tq, hp=hp),
        grid=(b, n_heads // hp, t // tq),
        in_specs=[pl.BlockSpec((1, tq, w), lambda i, h, j: (i, j, h)),
                  pl.BlockSpec((1, t, w), lambda i, h, j: (i, 0, h)),
                  pl.BlockSpec((1, t, w), lambda i, h, j: (i, 0, h)),
                  pl.BlockSpec((1, hp, tq), lambda i, h, j: (h, 0, 0))],
        out_specs=pl.BlockSpec((1, tq, w), lambda i, h, j: (i, j, h)),
        out_shape=jax.ShapeDtypeStruct((b, t, main_w), BF),
        compiler_params=_params(("parallel", "parallel", "arbitrary")),
        name="stick_breaking_prompt",
    )(proj, k_bf, v_bf, bias_b)


def _sb_dec_kernel(pt_ref, qt_ref, bias_ref, kn_ref, vn_ref, *rest, pages, n_q, n_new):
    k_refs = rest[:pages]
    v_refs = rest[pages:2 * pages]
    o_ref, tail_ref, acc_ref = rest[2 * pages:]
    j = pl.program_id(1)
    inv_sqrt = 1.0 / np.sqrt(np.float32(HEAD_DIM))
    qt = qt_ref[0]
    bias = bias_ref[...]
    n_cols = acc_ref.shape[0]
    nk = kn_ref.shape[1]
    n_heads = k_refs[0].shape[1]
    ss = lax.broadcasted_iota(I32, (nk, nk), 0)
    jj = lax.broadcasted_iota(I32, (nk, nk), 1)
    later = (jj > ss).astype(BF)

    def heads_cat(ref):
        return jnp.concatenate([ref[0, h] for h in range(n_heads)], axis=1).astype(BF)

    def scores(kcat, mask):
        z = _dot(kcat, qt) * inv_sqrt + bias
        sp = _softplus(z)
        if mask is not None:
            sp = jnp.where(mask, sp, 0.0)
        hi, lo = _split_bf16(sp)
        both = _dot(later, jnp.concatenate([hi, lo], axis=1))
        later_sum = both[:, :128] + both[:, 128:]
        return z, sp, later_sum, jnp.sum(sp, axis=0, keepdims=True)

    def weights(z, sp, later_sum, tail, mask):
        a = jnp.exp(z - sp - later_sum + tail)
        if mask is not None:
            a = jnp.where(mask, a, 0.0)
        return a.T[:n_cols].astype(BF)

    @pl.when(j == 0)
    def _():
        r = lax.broadcasted_iota(I32, (nk, 128), 0)
        c = lax.broadcasted_iota(I32, (nk, 128), 1)
        mask = (r < c % n_q) & (r < n_new)
        z, sp, later_sum, total = scores(kn_ref[0].astype(BF), mask)
        a_t = weights(z, sp, later_sum, jnp.zeros_like(total), mask)
        acc_ref[...] = _dot(a_t, vn_ref[0].astype(BF))
        tail_ref[...] = -total

    parts = [scores(heads_cat(k_refs[g]), None) for g in range(pages)]
    tail = tail_ref[...]
    a_ts = []
    for z, sp, later_sum, total in parts:
        a_ts.append(weights(z, sp, later_sum, tail, None))
        tail = tail - total
    v_all = jnp.concatenate([heads_cat(v_refs[g]) for g in range(pages)], axis=0)
    acc_ref[...] += _dot(jnp.concatenate(a_ts, axis=1), v_all)
    tail_ref[...] = tail

    @pl.when(j == pl.num_programs(1) - 1)
    def _():
        for h in range(o_ref.shape[-1] // HEAD_DIM):
            sl = slice(h * HEAD_DIM, (h + 1) * HEAD_DIM)
            o_ref[0, :, sl] = acc_ref[h * n_q:(h + 1) * n_q, sl].astype(o_ref.dtype)


def _stick_breaking_decode(q, k_new, v_new, cache_k, cache_v, page_table, bias, pages):
    b, t, main_w = q.shape
    n_pages = page_table.shape[1]
    _, page, n_heads, _ = cache_k.shape
    n_cols = n_heads * t
    qh = q.reshape(b, t, n_heads, HEAD_DIM).astype(F32)
    eye = jnp.eye(n_heads, dtype=F32)
    qt = jnp.einsum("bihd,hg->bhdgi", qh, eye).reshape(b, main_w, n_cols)
    qt = jnp.pad(qt, ((0, 0), (0, 0), (0, 128 - n_cols))).astype(BF)
    bias_c = jnp.pad(jnp.repeat(bias.astype(F32), t), (0, 128 - n_cols)).reshape(1, 128)
    new_rows = page
    k_pad = jnp.pad(k_new, ((0, 0), (0, new_rows - t), (0, 0)))
    v_pad = jnp.pad(v_new, ((0, 0), (0, new_rows - t), (0, 0)))

    def page_map(g):
        return lambda i, j, pt: (pt[i, n_pages - 1 - (j * pages + g)], 0, 0, 0)

    cache_k = jnp.transpose(cache_k, (0, 2, 1, 3))
    cache_v = jnp.transpose(cache_v, (0, 2, 1, 3))
    cache_specs = [pl.BlockSpec((1, n_heads, page, HEAD_DIM), page_map(g)) for g in range(pages)]
    grid_spec = pltpu.PrefetchScalarGridSpec(
        num_scalar_prefetch=1,
        grid=(b, n_pages // pages),
        in_specs=[pl.BlockSpec((1, main_w, 128), lambda i, j, pt: (i, 0, 0)),
                  pl.BlockSpec((1, 128), lambda i, j, pt: (0, 0)),
                  pl.BlockSpec((1, new_rows, main_w), lambda i, j, pt: (i, 0, 0)),
                  pl.BlockSpec((1, new_rows, main_w), lambda i, j, pt: (i, 0, 0))]
                 + cache_specs + cache_specs,
        out_specs=pl.BlockSpec((1, t, main_w), lambda i, j, pt: (i, 0, 0)),
        scratch_shapes=[pltpu.VMEM((1, 128), F32), pltpu.VMEM((n_cols, main_w), F32)],
    )
    return pl.pallas_call(
        functools.partial(_sb_dec_kernel, pages=pages, n_q=t, n_new=t),
        grid_spec=grid_spec,
        out_shape=jax.ShapeDtypeStruct((b, t, main_w), BF),
        compiler_params=_params(("parallel", "arbitrary")),
        name="stick_breaking_decode",
    )(page_table, qt, bias_c, k_pad, v_pad, *([cache_k] * pages), *([cache_v] * pages))


def _out_ln_kernel(ym_ref, ye_ref, x_ref, wa_ref, wb_ref, g_ref, b_ref, rc_ref, rh_ref,
                   x1_ref, lt_ref, hbuf, *, alpha):
    @pl.when(pl.program_id(0) == 0)
    def _():
        hbuf[...] = jnp.zeros_like(hbuf)

    tm, d = x_ref.shape
    chunks = d // LANES
    x1 = _layer_norm(alpha * x_ref[...] + hbuf[...], g_ref[...], b_ref[...])
    for c in range(chunks):
        x1_ref[pl.ds(c, tm, stride=chunks), :] = x1[:, c * LANES:(c + 1) * LANES]
    xh, xl = _split_bf16(x1)
    both = _dot(xh, rc_ref[...])
    lt = both[:, :LANES] + both[:, LANES:] + _dot(xl, rh_ref[...])
    lt_ref[...] = lt.T[:ROUTE_ROWS]
    hbuf[...] = _dot(ym_ref[...], wa_ref[...]) + _dot(ye_ref[...], wb_ref[...])


def _out_ln_route(y_main, y_mem, x, w_out_bf, layer, gamma, beta, r_cat, r_hi, alpha, tm):
    n, d = x.shape
    main_w = y_main.shape[1]
    mem_w = y_mem.shape[1]
    nt = n // tm
    chunks = d // LANES
    cur = lambda i: (jnp.minimum(i, nt - 1), 0)
    prev = lambda i: (jnp.maximum(i - 1, 0), 0)
    return pl.pallas_call(
        functools.partial(_out_ln_kernel, alpha=alpha),
        grid=(nt + 1,),
        in_specs=[pl.BlockSpec((tm, main_w), cur),
                  pl.BlockSpec((tm, mem_w), cur),
                  pl.BlockSpec((tm, d), prev),
                  pl.BlockSpec((None, main_w, d), lambda i: (layer, 0, 0)),
                  pl.BlockSpec((None, mem_w, d), lambda i: (layer, main_w // mem_w, 0)),
                  pl.BlockSpec((1, d), lambda i: (0, 0)),
                  pl.BlockSpec((1, d), lambda i: (0, 0)),
                  pl.BlockSpec((d, 2 * LANES), lambda i: (0, 0)),
                  pl.BlockSpec((d, LANES), lambda i: (0, 0))],
        out_specs=[pl.BlockSpec((tm * chunks, LANES), prev),
                   pl.BlockSpec((ROUTE_ROWS, tm), lambda i: (0, jnp.maximum(i - 1, 0)))],
        out_shape=[jax.ShapeDtypeStruct((n * chunks, LANES), F32),
                   jax.ShapeDtypeStruct((ROUTE_ROWS, n), F32)],
        scratch_shapes=[pltpu.VMEM((tm, d), F32)],
        compiler_params=_params(("arbitrary",)),
        name="out_ln_route",
    )(y_main, y_mem, x, w_out_bf, w_out_bf, gamma.reshape(1, d), beta.reshape(1, d), r_cat, r_hi)


def _route_kernel(lt_ref, rb_ref, e1_ref, e2_ref, g1_ref, g2_ref):
    lt = lt_ref[...] + rb_ref[...]
    ng, ne = N_GROUPS, EXP_PER_GROUP
    lg = [lt[g:g + 1, :] for g in range(ng)]
    m = lg[0]
    gi = jnp.zeros(m.shape, I32)
    for g in range(1, ng):
        better = lg[g] > m
        gi = jnp.where(better, g, gi)
        m = jnp.where(better, lg[g], m)
    den = jnp.exp(lg[0] - m)
    for g in range(1, ng):
        den = den + jnp.exp(lg[g] - m)
    p_sel = 1.0 / den
    le = []
    for e in range(ne):
        v = lt[ng + e:ng + e + 1, :]
        for g in range(1, ng):
            r = ng + g * ne + e
            v = jnp.where(gi == g, lt[r:r + 1, :], v)
        le.append(v)
    v1 = le[0]
    i1 = jnp.zeros(m.shape, I32)
    for e in range(1, ne):
        better = le[e] > v1
        i1 = jnp.where(better, e, i1)
        v1 = jnp.where(better, le[e], v1)
    v2 = jnp.full(m.shape, -jnp.inf, F32)
    i2 = jnp.zeros(m.shape, I32)
    for e in range(ne):
        better = (i1 != e) & (le[e] > v2)
        i2 = jnp.where(better, e, i2)
        v2 = jnp.where(better, le[e], v2)
    t = jnp.exp(v2 - v1)
    w_a = 1.0 / (1.0 + t)
    w_b = t / (1.0 + t)
    e1_ref[...] = gi * ne + i1
    e2_ref[...] = gi * ne + i2
    g1_ref[...] = p_sel * w_a
    g2_ref[...] = p_sel * w_b


def _route(lt, rbias):
    n = lt.shape[1]
    shp = lambda dt: jax.ShapeDtypeStruct((1, n), dt)
    return pl.pallas_call(
        _route_kernel,
        out_shape=[shp(I32), shp(I32), shp(F32), shp(F32)],
        compiler_params=pltpu.CompilerParams(vmem_limit_bytes=VMEM_LIMIT),
        name="route",
    )(lt, rbias)


def _moe_plan(e1, e2, tm, n_tiles):
    n = e1.shape[0]
    e = jnp.concatenate([e1, e2])
    oh = (e[:, None] == jnp.arange(N_EXPERTS, dtype=I32)[None, :]).astype(I32)
    csum = jnp.cumsum(oh, axis=0)
    rank = jnp.sum((csum - oh) * oh, axis=1)
    counts = csum[-1]
    tiles = (counts + tm - 1) // tm
    tile_end = jnp.cumsum(tiles)
    starts = (tile_end - tiles) * tm
    dest = jnp.sum(oh * starts[None, :], axis=1) + rank
    n_used = tile_end[-1]
    tidx = jnp.arange(n_tiles, dtype=I32)
    te = jnp.sum((tidx[:, None] >= tile_end[None, :]).astype(I32), axis=1)
    last_e = jnp.max(jnp.where(tiles > 0, jnp.arange(N_EXPERTS, dtype=I32), 0))
    te = jnp.where(tidx < n_used, te, last_e).astype(I32)
    rows = n_tiles * tm
    pair_of_row = jnp.full((rows,), -1, I32).at[dest].set(jnp.arange(2 * n, dtype=I32))
    has_pair = pair_of_row >= 0
    filler = jnp.logical_and(~has_pair, jnp.arange(rows, dtype=I32) < n_used * tm)
    spill = 2 * n + tm + jnp.cumsum(filler.astype(I32)) - 1
    src = jnp.where(has_pair, pair_of_row % n, 0).astype(I32)
    dst_rows = jnp.where(has_pair, pair_of_row, jnp.where(filler, spill, 2 * n))
    dst = jnp.concatenate([2 * n + jnp.arange(tm, dtype=I32), dst_rows]).astype(I32)
    return src, dst, te, n_used.reshape(1).astype(I32)


def _moe_kernel(te_ref, src_ref, dst_ref, nu_ref, x_hbm, wg_ref, wu_ref, wd_ref, y_hbm,
                xbuf, ybuf, gsem, ssem, wg_s, wu_s, wd_s, *, tm, n_pairs, n_spill_tiles):
    j = pl.program_id(0)
    n_used = nu_ref[0]
    slot = j % 2
    other = 1 - slot
    chunks = wg_s.shape[0] // LANES
    tile_rows = tm * chunks

    def token_rows(token):
        return pl.ds(pl.multiple_of(token * chunks, chunks), chunks)

    def slot_rows(s):
        return pl.ds(pl.multiple_of(s * tile_rows, tile_rows), tile_rows)

    def start_gather(tile, to_slot):
        for r in range(tm):
            pltpu.make_async_copy(x_hbm.at[token_rows(src_ref[tile * tm + r])],
                                  xbuf.at[token_rows(to_slot * tm + r)], gsem.at[to_slot]).start()

    def wait_gather(of_slot):
        pltpu.make_async_copy(x_hbm.at[pl.ds(0, tile_rows)], xbuf.at[slot_rows(of_slot)],
                              gsem.at[of_slot]).wait()

    def start_scatter(tile, from_slot):
        for r in range(tm):
            pltpu.make_async_copy(ybuf.at[token_rows(from_slot * tm + r)],
                                  y_hbm.at[token_rows(dst_ref[(tile + 1) * tm + r])],
                                  ssem.at[from_slot]).start()

    def wait_scatter(of_slot):
        pltpu.make_async_copy(ybuf.at[slot_rows(of_slot)], y_hbm.at[pl.ds(0, tile_rows)],
                              ssem.at[of_slot]).wait()

    @pl.when(j == 0)
    def _():
        start_gather(0, 0)
        ybuf[pl.ds(tile_rows, tile_rows), :] = jnp.zeros((tile_rows, LANES), F32)
        fills = [pltpu.make_async_copy(ybuf.at[pl.ds(tile_rows, tile_rows)],
                                       y_hbm.at[pl.ds((n_pairs + c * tm) * chunks, tile_rows)],
                                       ssem.at[1]) for c in range(1, n_spill_tiles)]
        for f in fills:
            f.start()
        for f in fills:
            f.wait()

    e = te_ref[j]
    e_prev = te_ref[jnp.maximum(j - 1, 0)]

    @pl.when((j == 0) | (e != e_prev))
    def _():
        wg_s[...] = wg_ref[0, 0].astype(BF)
        wu_s[...] = wu_ref[0, 0].astype(BF)
        wd_s[...] = wd_ref[0, 0].astype(BF)

    @pl.when((j >= 1) & (j <= n_used))
    def _():
        wait_scatter(slot)

    @pl.when(j < n_used)
    def _():
        wait_gather(slot)
        start_gather(j + 1, other)
        start_scatter(j - 1, other)
        base = slot * tile_rows
        x = jnp.concatenate([xbuf[pl.ds(base + c, tm, stride=chunks), :] for c in range(chunks)],
                            axis=1).astype(BF)
        g = _dot(x, wg_s[...])
        u = _dot(x, wu_s[...])
        h = (g * jax.nn.sigmoid(g) * u).astype(BF)
        y = _dot(h, wd_s[...])
        for c in range(chunks):
            ybuf[pl.ds(base + c, tm, stride=chunks), :] = y[:, c * LANES:(c + 1) * LANES]

    @pl.when(j == n_used)
    def _():
        wait_gather(slot)
        start_scatter(j - 1, other)
        wait_scatter(other)


def _moe_experts(x1, src, dst, te, n_used, w_gate, w_up, w_down, layer, tm, n_tiles):
    d, ff = w_gate.shape[-2:]
    chunks = d // LANES
    n = x1.shape[0] // chunks
    n_spill_tiles = 1 + -(-N_EXPERTS * (tm - 1) // tm)
    grid_spec = pltpu.PrefetchScalarGridSpec(
        num_scalar_prefetch=4,
        grid=(n_tiles,),
        in_specs=[pl.BlockSpec(memory_space=pl.ANY),
                  pl.BlockSpec((1, 1, d, ff), lambda j, te, *_: (layer, te[j], 0, 0)),
                  pl.BlockSpec((1, 1, d, ff), lambda j, te, *_: (layer, te[j], 0, 0)),
                  pl.BlockSpec((1, 1, ff, d), lambda j, te, *_: (layer, te[j], 0, 0))],
        out_specs=pl.BlockSpec(memory_space=pl.ANY),
        scratch_shapes=[pltpu.VMEM((2 * tm * chunks, LANES), F32),
                        pltpu.VMEM((2 * tm * chunks, LANES), F32),
                        pltpu.SemaphoreType.DMA((2,)),
                        pltpu.SemaphoreType.DMA((2,)),
                        pltpu.VMEM((d, ff), BF),
                        pltpu.VMEM((d, ff), BF),
                        pltpu.VMEM((ff, d), BF)],
    )
    return pl.pallas_call(
        functools.partial(_moe_kernel, tm=tm, n_pairs=2 * n, n_spill_tiles=n_spill_tiles),
        grid_spec=grid_spec,
        out_shape=jax.ShapeDtypeStruct(((2 * n + n_spill_tiles * tm) * chunks, LANES), F32),
        compiler_params=_params(("arbitrary",)),
        name="moe_experts",
    )(te, src, dst, n_used, x1, w_gate, w_up, w_down)


def _combine_kernel(x1_ref, ya_ref, yb_ref, g1_ref, g2_ref, gam_ref, bet_ref,
                    x2_ref, x2b_ref, *, alpha):
    tm, d = x2_ref.shape
    chunks = d // LANES
    g1 = g1_ref[...]
    g2 = g2_ref[...]
    cols = []
    for c in range(chunks):
        rows = pl.ds(c, tm, stride=chunks)
        cols.append(alpha * x1_ref[rows, :] + g1 * ya_ref[rows, :] + g2 * yb_ref[rows, :])
    x2 = _layer_norm(jnp.concatenate(cols, axis=1), gam_ref[...], bet_ref[...])
    x2_ref[...] = x2
    x2b_ref[...] = x2.astype(BF)


def _combine_ln(x1, y_pairs, g1, g2, gamma, beta, alpha, tm):
    d = gamma.shape[0]
    chunks = d // LANES
    n = x1.shape[0] // chunks
    nb = n // tm
    return pl.pallas_call(
        functools.partial(_combine_kernel, alpha=alpha),
        grid=(nb,),
        in_specs=[pl.BlockSpec((tm * chunks, LANES), lambda i: (i, 0)),
                  pl.BlockSpec((tm * chunks, LANES), lambda i: (i, 0)),
                  pl.BlockSpec((tm * chunks, LANES), lambda i: (i + nb, 0)),
                  pl.BlockSpec((tm, 1), lambda i: (i, 0)),
                  pl.BlockSpec((tm, 1), lambda i: (i, 0)),
                  pl.BlockSpec((1, d), lambda i: (0, 0)),
                  pl.BlockSpec((1, d), lambda i: (0, 0))],
        out_specs=[pl.BlockSpec((tm, d), lambda i: (i, 0)),
                   pl.BlockSpec((tm, d), lambda i: (i, 0))],
        out_shape=[jax.ShapeDtypeStruct((n, d), F32), jax.ShapeDtypeStruct((n, d), BF)],
        compiler_params=_params(("parallel",)),
        name="combine_ln",
    )(x1, y_pairs, y_pairs, g1.reshape(n, 1), g2.reshape(n, 1), gamma.reshape(1, d),
      beta.reshape(1, d))


def _hier_moe_ln(x1, lt, rbias, w_gate, w_up, w_down, layer, gamma, beta, alpha, tm_moe, tm_tok):
    n = lt.shape[1]
    e1, e2, g1, g2 = _route(lt, rbias)
    n_tiles = (2 * n + N_EXPERTS * (tm_moe - 1)) // tm_moe + 1
    src, dst, te, n_used = _moe_plan(e1[0], e2[0], tm_moe, n_tiles)
    y_pairs = _moe_experts(x1, src, dst, te, n_used, w_gate, w_up, w_down, layer, tm_moe, n_tiles)
    return _combine_ln(x1, y_pairs, g1[0], g2[0], gamma, beta, alpha, tm_tok)


def _trunk(x3, pool_prev, mem_k, mem_v, sb_past, p, *, decode):
    b, t, d = x3.shape
    depth = p["w_in"].shape[0]
    n_a = p["w_pool"].shape[0]
    main_w = p["w_sb_kv"].shape[-1] // 2
    mem_w = d - main_w
    n_heads = main_w // HEAD_DIM
    alpha = float((2 * depth) ** 0.25)
    n = b * t
    tm = min(512, n)
    tm_c = min(256, n)
    tm_moe = 256 if n >= 2048 else 16
    pos0 = 0 if not decode else sb_past[2].shape[1] * sb_past[0].shape[1]

    x = x3.reshape(n, d)
    xb = x
    new_pool = []
    k_bf = v_bf = k_f = v_f = None
    for l in range(depth):
        (proj,) = _matmul(xb, p["w_in"], l, [(0, d, F32)], tm, "proj_in")
        proj3 = proj.reshape(b, t, d)
        main = proj3[..., :main_w]
        if l < n_a:
            if decode:
                u_tm = jnp.transpose(main, (1, 0, 2))
                prev_tm = jnp.transpose(pool_prev[l], (1, 0, 2))
                y_tm = _pool_mix_decode(u_tm, prev_tm, p["w_pool"][l], p["pool_scale"][l], pos0)
                y_main = jnp.transpose(y_tm, (1, 0, 2))
            else:
                y_main = _pool_mix(proj3, pool_prev[l], p["w_pool"][l], p["pool_scale"][l],
                                   main_w, min(256, t), pos0)
            if t >= POOL_STATE:
                new_pool.append(main[:, t - POOL_STATE:, :])
            else:
                new_pool.append(jnp.concatenate([pool_prev[l][:, t:, :], main], axis=1))
        else:
            bias = p["sb_bias"][l - n_a]
            if decode:
                y_main = _stick_breaking_decode(main, k_f.reshape(b, t, main_w),
                                                v_f.reshape(b, t, main_w),
                                                sb_past[0], sb_past[1], sb_past[2], bias, pages=4)
            else:
                y_main = _stick_breaking_prompt(proj3, k_bf.reshape(b, t, main_w),
                                                v_bf.reshape(b, t, main_w), bias, n_heads, 256, 4)
        if decode:
            t_pad = 8
            proj_pad = jnp.pad(proj3, ((0, 0), (0, t_pad - t), (0, 0)))
            y_mem = _mem_attend(proj_pad, mem_k[l], mem_v[l], mem_w, t_pad)[:, :t]
        else:
            y_mem = _mem_attend(proj3, mem_k[l], mem_v[l], mem_w, min(512, t))
        x1, lt = _out_ln_route(y_main.reshape(n, main_w), y_mem.reshape(n, mem_w), x,
                               p["w_out"], l, p["ln1_g"][l], p["ln1_b"][l],
                               p["r_cat"][l], p["r_hi"][l], alpha, tm_c)
        x, xb = _hier_moe_ln(x1, lt, p["r_bias"][l], p["w_gate"], p["w_up"], p["w_down"], l,
                             p["ln2_g"][l], p["ln2_b"][l], alpha, tm_moe, tm)
        if l == n_a - 1:
            if decode:
                k_f, v_f = _matmul(xb, p["w_sb_kv"], 0, [(0, main_w, F32), (main_w, main_w, F32)],
                                   n, "proj_sb_kv")
                k_out = k_f.reshape(b, t, n_heads, HEAD_DIM)
                v_out = v_f.reshape(b, t, n_heads, HEAD_DIM)
            else:
                k_hm, v_hm, k_bf, v_bf = _kv_project(xb, p["w_sb_kv"][0], b, t, 256)
                k_out = jnp.transpose(k_hm, (0, 2, 1, 3))
                v_out = jnp.transpose(v_hm, (0, 2, 1, 3))
    return x.reshape(b, t, d), jnp.stack(new_pool, axis=0), k_out, v_out


def kernel(x_prompt, x_sample, state_pool, cache_sb_k, cache_sb_v, cache_mem_k, cache_mem_v,
           page_table, mem_prompt, w_in, w_out, w_pool_grp, pool_scale, w_mem_k, w_mem_v,
           ln1_g, ln1_b, ln2_g, ln2_b, w_route_grp, b_route_grp, w_route_exp, b_route_exp,
           w_gate, w_up, w_down, w_sb_k, w_sb_v, sb_bias):
    depth, d, _ = w_in.shape
    bp, _, _ = x_prompt.shape
    bs, ts, _ = x_sample.shape
    n_a = w_pool_grp.shape[0]
    main_w = w_sb_k.shape[1]
    mem_w = w_mem_k.shape[2]
    n_mem = mem_prompt.shape[1]
    n_heads = main_w // HEAD_DIM
    mem_heads = mem_w // HEAD_DIM

    wr = jnp.concatenate([w_route_grp, w_route_exp.reshape(depth, d, N_EXPERTS)], axis=-1)
    wr = jnp.pad(wr, ((0, 0), (0, 0), (0, LANES - N_GROUPS - N_EXPERTS)))
    r_hi = wr.astype(BF)
    r_lo = (wr - r_hi.astype(F32)).astype(BF)
    r_cat = jnp.concatenate([r_hi, r_lo], axis=-1)
    r_bias = jnp.pad(jnp.concatenate([b_route_grp, b_route_exp.reshape(depth, N_EXPERTS)], axis=-1),
                     ((0, 0), (0, ROUTE_ROWS - N_GROUPS - N_EXPERTS))).reshape(depth, ROUTE_ROWS, 1)
    p = dict(
        w_in=w_in.astype(BF), w_out=w_out.astype(BF), w_pool=w_pool_grp.astype(BF),
        pool_scale=pool_scale, w_sb_kv=jnp.concatenate([w_sb_k, w_sb_v], axis=1).astype(BF)[None],
        ln1_g=ln1_g, ln1_b=ln1_b, ln2_g=ln2_g, ln2_b=ln2_b, r_hi=r_hi, r_cat=r_cat, r_bias=r_bias,
        w_gate=w_gate, w_up=w_up, w_down=w_down, sb_bias=sb_bias)

    w_mem = jnp.concatenate([jnp.transpose(w_mem_k, (1, 0, 2)).reshape(d, depth * mem_w),
                             jnp.transpose(w_mem_v, (1, 0, 2)).reshape(d, depth * mem_w)],
                            axis=1).astype(BF)
    mem_kv = _mem_project(mem_prompt.reshape(bp * n_mem, d).astype(BF), w_mem, mem_w)
    mem_kv = mem_kv.reshape(2, depth, bp, n_mem, mem_w)
    pool0 = jnp.zeros((n_a, bp, POOL_STATE, main_w), x_prompt.dtype)
    y_p, pool_p, k_p, v_p = _trunk(x_prompt, pool0, mem_kv[0], mem_kv[1], None, p, decode=False)

    sb_past = (cache_sb_k, cache_sb_v, page_table)
    y_s, pool_s, k_s, v_s = _trunk(
        x_sample, state_pool, cache_mem_k.reshape(depth, bs, n_mem, mem_w),
        cache_mem_v.reshape(depth, bs, n_mem, mem_w), sb_past, p, decode=True)

    return (y_p, y_s, pool_p, pool_s, k_p, v_p, k_s, v_s,
            mem_kv[0].reshape(depth, bp, n_mem, mem_heads, HEAD_DIM),
            mem_kv[1].reshape(depth, bp, n_mem, mem_heads, HEAD_DIM))
```

```python
import functools

import jax
import jax.numpy as jnp
import numpy as np
from jax import lax
from jax.experimental import pallas as pl
from jax.experimental.pallas import tpu as pltpu

F32 = jnp.float32
BF = jnp.bfloat16
I32 = jnp.int32

HEAD_DIM = 128
LANES = 128
POOL_WINDOWS = (2, 4, 8, 16)
POOL_STATE = max(POOL_WINDOWS) - 1
N_GROUPS = 4
EXP_PER_GROUP = 4
N_EXPERTS = N_GROUPS * EXP_PER_GROUP
LN_EPS = 1e-5
ROUTE_ROWS = 32
VMEM_LIMIT = 56 * 2**20

_NT = (((1,), (1,)), ((), ()))


def _params(sem):
    return pltpu.CompilerParams(dimension_semantics=sem, vmem_limit_bytes=VMEM_LIMIT)


def _dot(a, b):
    return jnp.dot(a, b, preferred_element_type=F32)


def _layer_norm(v, g, b):
    mu = jnp.mean(v, axis=-1, keepdims=True)
    c = v - mu
    var = jnp.mean(c * c, axis=-1, keepdims=True)
    return c * lax.rsqrt(var + LN_EPS) * g + b


def _softplus(z):
    return jnp.maximum(z, 0.0) + jnp.log(1.0 + jnp.exp(jnp.minimum(z, -z)))


def _split_bf16(x):
    hi = x.astype(BF)
    lo = (x - hi.astype(F32)).astype(BF)
    return hi, lo


def _mm_kernel(x_ref, w_ref, *o_refs, offsets):
    acc = _dot(x_ref[...].astype(BF), w_ref[...])
    for o_ref, off in zip(o_refs, offsets):
        o_ref[...] = acc[:, off:off + o_ref.shape[-1]].astype(o_ref.dtype)


def _matmul(x, w, layer, outs, tm, name):
    m, k = x.shape
    n = w.shape[-1]
    return pl.pallas_call(
        functools.partial(_mm_kernel, offsets=tuple(o for o, _, _ in outs)),
        grid=(m // tm,),
        in_specs=[pl.BlockSpec((tm, k), lambda i: (i, 0)),
                  pl.BlockSpec((None, k, n), lambda i: (layer, 0, 0))],
        out_specs=[pl.BlockSpec((tm, c), lambda i: (i, 0)) for _, c, _ in outs],
        out_shape=[jax.ShapeDtypeStruct((m, c), dt) for _, c, dt in outs],
        compiler_params=_params(("parallel",)),
        name=name,
    )(x, w)


def _kv_kernel(x_ref, w_ref, kf_ref, vf_ref, kb_ref, vb_ref):
    n_heads = kf_ref.shape[1]
    main_w = n_heads * HEAD_DIM
    acc = _dot(x_ref[...], w_ref[...])
    for h in range(n_heads):
        kf_ref[0, h] = acc[:, h * HEAD_DIM:(h + 1) * HEAD_DIM]
        vf_ref[0, h] = acc[:, main_w + h * HEAD_DIM:main_w + (h + 1) * HEAD_DIM]
    kb_ref[...] = acc[:, :main_w].astype(BF)
    vb_ref[...] = acc[:, main_w:].astype(BF)


def _kv_project(xb, w_kv, b, t, tm):
    n, d = xb.shape
    main_w = w_kv.shape[1] // 2
    n_heads = main_w // HEAD_DIM
    nt = t // tm
    head_major = pl.BlockSpec((1, n_heads, tm, HEAD_DIM), lambda i: (i // nt, 0, i % nt, 0))
    return pl.pallas_call(
        _kv_kernel,
        grid=(n // tm,),
        in_specs=[pl.BlockSpec((tm, d), lambda i: (i, 0)),
                  pl.BlockSpec((d, 2 * main_w), lambda i: (0, 0))],
        out_specs=[head_major, head_major,
                   pl.BlockSpec((tm, main_w), lambda i: (i, 0)),
                   pl.BlockSpec((tm, main_w), lambda i: (i, 0))],
        out_shape=[jax.ShapeDtypeStruct((b, n_heads, t, HEAD_DIM), F32)] * 2
                  + [jax.ShapeDtypeStruct((n, main_w), BF)] * 2,
        compiler_params=_params(("parallel",)),
        name="proj_sb_kv",
    )(xb, w_kv)


def _memproj_kernel(x_ref, w_ref, o_ref):
    o_ref[0] = _dot(x_ref[...], w_ref[...])


def _mem_project(mem_bf, w_cat, width):
    m, d = mem_bf.shape
    nj = w_cat.shape[1] // width
    return pl.pallas_call(
        _memproj_kernel,
        grid=(nj,),
        in_specs=[pl.BlockSpec((m, d), lambda j: (0, 0)),
                  pl.BlockSpec((d, width), lambda j: (0, j))],
        out_specs=pl.BlockSpec((1, m, width), lambda j: (j, 0, 0)),
        out_shape=jax.ShapeDtypeStruct((nj, m, width), F32),
        compiler_params=_params(("parallel",)),
        name="mem_project",
    )(mem_bf, w_cat)


def _pool_kernel(u_ref, prev_ref, w_ref, sc_ref, y_ref, buf, *, tt, nt, pos0):
    t = pl.program_id(1)
    p = POOL_STATE
    cg = w_ref.shape[-1]

    @pl.when(t == 0)
    def _():
        buf[1:1 + p, :] = prev_ref[0]

    u = u_ref[0]
    buf[p + 1:p + 1 + tt, :] = u
    pos = lax.broadcasted_iota(I32, (tt, 1), 0) + (t * tt + pos0)
    for g, w in enumerate(POOL_WINDOWS):
        c0 = g * cg
        ug = u[:, c0:c0 + cg]
        s = ug
        for k in range(1, w):
            s = s + buf[p + 1 - k:p + 1 - k + tt, c0:c0 + cg]
        cnt = jnp.minimum(w, pos + 1).astype(F32)
        d = (s / cnt - ug).astype(BF)
        yg = _dot(d, w_ref[g]) * sc_ref[:, c0:c0 + cg]
        y_ref[0, :, c0:c0 + cg] = yg.astype(y_ref.dtype)
    if nt > 1:
        buf[1:1 + p, :] = buf[tt + 1:tt + 1 + p, :]


def _pool_mix(proj, prev, w_grp_bf, scale, main_w, tt, pos0):
    b, t, _ = proj.shape
    nt = t // tt
    cg = main_w // len(POOL_WINDOWS)
    return pl.pallas_call(
        functools.partial(_pool_kernel, tt=tt, nt=nt, pos0=pos0),
        grid=(b, nt),
        in_specs=[pl.BlockSpec((1, tt, main_w), lambda i, j: (i, j, 0)),
                  pl.BlockSpec((1, POOL_STATE, main_w), lambda i, j: (i, 0, 0)),
                  pl.BlockSpec((len(POOL_WINDOWS), cg, cg), lambda i, j: (0, 0, 0)),
                  pl.BlockSpec((1, main_w), lambda i, j: (0, 0))],
        out_specs=pl.BlockSpec((1, tt, main_w), lambda i, j: (i, j, 0)),
        out_shape=jax.ShapeDtypeStruct((b, t, main_w), BF),
        scratch_shapes=[pltpu.VMEM((POOL_STATE + 1 + tt, main_w), F32)],
        compiler_params=_params(("parallel", "arbitrary")),
        name="pool_mix",
    )(proj, prev, w_grp_bf, scale.reshape(1, main_w))


def _pool_dec_kernel(u_ref, prev_ref, w_ref, sc_ref, y_ref, *, pos0):
    p = POOL_STATE
    nt = u_ref.shape[0]
    cg = w_ref.shape[-1]
    cat = [prev_ref[k] for k in range(p)] + [u_ref[i] for i in range(nt)]
    for g, w in enumerate(POOL_WINDOWS):
        c0 = g * cg
        ds = []
        for i in range(nt):
            s = cat[p + i][:, c0:c0 + cg]
            for k in range(1, w):
                s = s + cat[p + i - k][:, c0:c0 + cg]
            cnt = float(min(w, pos0 + i + 1))
            ds.append(s / cnt - cat[p + i][:, c0:c0 + cg])
        d = jnp.concatenate(ds, axis=0).astype(BF)
        yg = _dot(d, w_ref[g]) * sc_ref[:, c0:c0 + cg]
        nb = ds[0].shape[0]
        for i in range(nt):
            y_ref[i, :, c0:c0 + cg] = yg[i * nb:(i + 1) * nb].astype(y_ref.dtype)


def _pool_mix_decode(u_tm, prev_tm, w_grp_bf, scale, pos0):
    nt, b, c = u_tm.shape
    return pl.pallas_call(
        functools.partial(_pool_dec_kernel, pos0=pos0),
        out_shape=jax.ShapeDtypeStruct((nt, b, c), BF),
        compiler_params=pltpu.CompilerParams(vmem_limit_bytes=VMEM_LIMIT),
        name="pool_mix_decode",
    )(u_tm, prev_tm, w_grp_bf, scale.reshape(1, c))


def _mem_kernel(q_ref, k_ref, v_ref, o_ref, *, cdt):
    nh = q_ref.shape[-1] // HEAD_DIM
    scale = HEAD_DIM ** -0.5
    for h in range(nh):
        sl = slice(h * HEAD_DIM, (h + 1) * HEAD_DIM)
        q = q_ref[0, :, sl].astype(cdt)
        k = k_ref[0, :, sl].astype(cdt)
        v = v_ref[0, :, sl].astype(cdt)
        s = lax.dot_general(q, k, _NT, preferred_element_type=F32) * scale
        e = jnp.exp(s - jnp.max(s, axis=-1, keepdims=True))
        den = jnp.sum(e, axis=-1, keepdims=True)
        o = _dot(e.astype(cdt), v) / den
        o_ref[0, :, sl] = o.astype(o_ref.dtype)


def _mem_attend(proj, mk, mv, mem_w, tt):
    b, t, mix_w = proj.shape
    n_mem = mk.shape[1]
    qblk = (mix_w - mem_w) // mem_w
    cdt = BF if tt % 16 == 0 else F32
    return pl.pallas_call(
        functools.partial(_mem_kernel, cdt=cdt),
        grid=(b, t // tt),
        in_specs=[pl.BlockSpec((1, tt, mem_w), lambda i, j: (i, j, qblk)),
                  pl.BlockSpec((1, n_mem, mem_w), lambda i, j: (i, 0, 0)),
                  pl.BlockSpec((1, n_mem, mem_w), lambda i, j: (i, 0, 0))],
        out_specs=pl.BlockSpec((1, tt, mem_w), lambda i, j: (i, j, 0)),
        out_shape=jax.ShapeDtypeStruct((b, t, mem_w), BF),
        compiler_params=_params(("parallel", "parallel")),
        name="mem_attend",
    )(proj, mk, mv)


def _sb_kernel(q_ref, k_ref, v_ref, bias_ref, o_ref, *, tq, hp):
    qi = pl.program_id(2)
    inv_sqrt = 1.0 / np.sqrt(np.float32(HEAD_DIM))
    jj = lax.broadcasted_iota(I32, (tq, tq), 0)
    ss = lax.broadcasted_iota(I32, (tq, tq), 1)
    later = (jj > ss).astype(BF)
    later2 = jnp.concatenate([later, later], axis=0)
    causal = ss < jj
    heads = [slice(h * HEAD_DIM, (h + 1) * HEAD_DIM) for h in range(hp)]
    qs = [(q_ref[0, :, sl] * inv_sqrt).astype(BF) for sl in heads]
    biases = [bias_ref[0, h:h + 1, :] for h in range(hp)]

    def blocks(start, state, mask):
        rng = range(hp)
        zs = [lax.dot_general(qs[h], k_ref[0, pl.ds(start, tq), heads[h]], _NT,
                              preferred_element_type=F32) + biases[h] for h in rng]
        sps = [_softplus(z) for z in zs]
        if mask is not None:
            sps = [jnp.where(mask, sp, 0.0) for sp in sps]
        splits = [_split_bf16(sp) for sp in sps]
        later_sums = [_dot(jnp.concatenate([hi, lo], axis=1), later2) for hi, lo in splits]
        ws = [jnp.exp(zs[h] - sps[h] - later_sums[h] + state[h][0]) for h in rng]
        if mask is not None:
            ws = [jnp.where(mask, a, 0.0) for a in ws]
        accs = [state[h][1] + _dot(ws[h].astype(BF), v_ref[0, pl.ds(start, tq), heads[h]])
                for h in rng]
        tails = [state[h][0] - jnp.sum(sps[h], axis=1, keepdims=True) for h in rng]
        return tuple(zip(tails, accs))

    diag = pl.multiple_of(qi * tq, tq)
    zero = (jnp.zeros((tq, 1), F32), jnp.zeros((tq, HEAD_DIM), F32))
    state = blocks(diag, (zero,) * hp, causal)

    def body(n, carry):
        return blocks(pl.multiple_of((qi - 1 - n) * tq, tq), carry, None)

    state = lax.fori_loop(0, qi, body, state)
    for h in range(hp):
        o_ref[0, :, heads[h]] = state[h][1].astype(o_ref.dtype)


def _stick_breaking_prompt(proj, k_bf, v_bf, bias, n_heads, tq, hp):
    b, t, _ = proj.shape
    main_w = n_heads * HEAD_DIM
    w = hp * HEAD_DIM
    bias_b = jnp.broadcast_to(bias.astype(F32).reshape(n_heads // hp, hp, 1),
                              (n_heads // hp, hp, tq))
    return pl.pallas_call(
        functools.partial(_sb_kernel, tq=tq, hp=hp),
        grid=(b, n_heads // hp, t // tq),
        in_specs=[pl.BlockSpec((1, tq, w), lambda i, h, j: (i, j, h)),
                  pl.BlockSpec((1, t, w), lambda i, h, j: (i, 0, h)),
                  pl.BlockSpec((1, t, w), lambda i, h, j: (i, 0, h)),
                  pl.BlockSpec((1, hp, tq), lambda i, h, j: (h, 0, 0))],
        out_specs=pl.BlockSpec((1, tq, w), lambda i, h, j: (i, j, h)),
        out_shape=jax.ShapeDtypeStruct((b, t, main_w), BF),
        compiler_params=_params(("parallel", "parallel", "arbitrary")),
        name="stick_breaking_prompt",
    )(proj, k_bf, v_bf, bias_b)


def _sb_dec_kernel(pt_ref, qt_ref, bias_ref, kn_ref, vn_ref, *rest, pages, n_q, n_new):
    k_refs = rest[:pages]
    v_refs = rest[pages:2 * pages]
    o_ref, tail_ref, acc_ref = rest[2 * pages:]
    j = pl.program_id(1)
    inv_sqrt = 1.0 / np.sqrt(np.float32(HEAD_DIM))
    qt = qt_ref[0]
    bias = bias_ref[...]
    n_cols = acc_ref.shape[0]
    nk = kn_ref.shape[1]
    n_heads = k_refs[0].shape[1]
    ss = lax.broadcasted_iota(I32, (nk, nk), 0)
    jj = lax.broadcasted_iota(I32, (nk, nk), 1)
    later = (jj > ss).astype(BF)

    def heads_cat(ref):
        return jnp.concatenate([ref[0, h] for h in range(n_heads)], axis=1).astype(BF)

    def scores(kcat, mask):
        z = _dot(kcat, qt) * inv_sqrt + bias
        sp = _softplus(z)
        if mask is not None:
            sp = jnp.where(mask, sp, 0.0)
        hi, lo = _split_bf16(sp)
        both = _dot(later, jnp.concatenate([hi, lo], axis=1))
        later_sum = both[:, :128] + both[:, 128:]
        return z, sp, later_sum, jnp.sum(sp, axis=0, keepdims=True)

    def weights(z, sp, later_sum, tail, mask):
        a = jnp.exp(z - sp - later_sum + tail)
        if mask is not None:
            a = jnp.where(mask, a, 0.0)
        return a.T[:n_cols].astype(BF)

    @pl.when(j == 0)
    def _():
        r = lax.broadcasted_iota(I32, (nk, 128), 0)
        c = lax.broadcasted_iota(I32, (nk, 128), 1)
        mask = (r < c % n_q) & (r < n_new)
        z, sp, later_sum, total = scores(kn_ref[0].astype(BF), mask)
        a_t = weights(z, sp, later_sum, jnp.zeros_like(total), mask)
        acc_ref[...] = _dot(a_t, vn_ref[0].astype(BF))
        tail_ref[...] = -total

    parts = [scores(heads_cat(k_refs[g]), None) for g in range(pages)]
    tail = tail_ref[...]
    a_ts = []
    for z, sp, later_sum, total in parts:
        a_ts.append(weights(z, sp, later_sum, tail, None))
        tail = tail - total
    v_all = jnp.concatenate([heads_cat(v_refs[g]) for g in range(pages)], axis=0)
    acc_ref[...] += _dot(jnp.concatenate(a_ts, axis=1), v_all)
    tail_ref[...] = tail

    @pl.when(j == pl.num_programs(1) - 1)
    def _():
        for h in range(o_ref.shape[-1] // HEAD_DIM):
            sl = slice(h * HEAD_DIM, (h + 1) * HEAD_DIM)
            o_ref[0, :, sl] = acc_ref[h * n_q:(h + 1) * n_q, sl].astype(o_ref.dtype)


def _stick_breaking_decode(q, k_new, v_new, cache_k, cache_v, page_table, bias, pages):
    b, t, main_w = q.shape
    n_pages = page_table.shape[1]
    _, page, n_heads, _ = cache_k.shape
    n_cols = n_heads * t
    qh = q.reshape(b, t, n_heads, HEAD_DIM).astype(F32)
    eye = jnp.eye(n_heads, dtype=F32)
    qt = jnp.einsum("bihd,hg->bhdgi", qh, eye).reshape(b, main_w, n_cols)
    qt = jnp.pad(qt, ((0, 0), (0, 0), (0, 128 - n_cols))).astype(BF)
    bias_c = jnp.pad(jnp.repeat(bias.astype(F32), t), (0, 128 - n_cols)).reshape(1, 128)
    new_rows = page
    k_pad = jnp.pad(k_new, ((0, 0), (0, new_rows - t), (0, 0)))
    v_pad = jnp.pad(v_new, ((0, 0), (0, new_rows - t), (0, 0)))

    def page_map(g):
        return lambda i, j, pt: (pt[i, n_pages - 1 - (j * pages + g)], 0, 0, 0)

    cache_k = jnp.transpose(cache_k, (0, 2, 1, 3))
    cache_v = jnp.transpose(cache_v, (0, 2, 1, 3))
    cache_specs = [pl.BlockSpec((1, n_heads, page, HEAD_DIM), page_map(g)) for g in range(pages)]
    grid_spec = pltpu.PrefetchScalarGridSpec(
        num_scalar_prefetch=1,
        grid=(b, n_pages // pages),
        in_specs=[pl.BlockSpec((1, main_w, 128), lambda i, j, pt: (i, 0, 0)),
                  pl.BlockSpec((1, 128), lambda i, j, pt: (0, 0)),
                  pl.BlockSpec((1, new_rows, main_w), lambda i, j, pt: (i, 0, 0)),
                  pl.BlockSpec((1, new_rows, main_w), lambda i, j, pt: (i, 0, 0))]
                 + cache_specs + cache_specs,
        out_specs=pl.BlockSpec((1, t, main_w), lambda i, j, pt: (i, 0, 0)),
        scratch_shapes=[pltpu.VMEM((1, 128), F32), pltpu.VMEM((n_cols, main_w), F32)],
    )
    return pl.pallas_call(
        functools.partial(_sb_dec_kernel, pages=pages, n_q=t, n_new=t),
        grid_spec=grid_spec,
        out_shape=jax.ShapeDtypeStruct((b, t, main_w), BF),
        compiler_params=_params(("parallel", "arbitrary")),
        name="stick_breaking_decode",
    )(page_table, qt, bias_c, k_pad, v_pad, *([cache_k] * pages), *([cache_v] * pages))


def _out_ln_kernel(ym_ref, ye_ref, x_ref, wa_ref, wb_ref, g_ref, b_ref, rc_ref, rh_ref,
                   x1_ref, lt_ref, hbuf, *, alpha):
    @pl.when(pl.program_id(0) == 0)
    def _():
        hbuf[...] = jnp.zeros_like(hbuf)

    tm, d = x_ref.shape
    chunks = d // LANES
    x1 = _layer_norm(alpha * x_ref[...] + hbuf[...], g_ref[...], b_ref[...])
    for c in range(chunks):
        x1_ref[pl.ds(c, tm, stride=chunks), :] = x1[:, c * LANES:(c + 1) * LANES]
    xh, xl = _split_bf16(x1)
    both = _dot(xh, rc_ref[...])
    lt = both[:, :LANES] + both[:, LANES:] + _dot(xl, rh_ref[...])
    lt_ref[...] = lt.T[:ROUTE_ROWS]
    hbuf[...] = _dot(ym_ref[...], wa_ref[...]) + _dot(ye_ref[...], wb_ref[...])


def _out_ln_route(y_main, y_mem, x, w_out_bf, layer, gamma, beta, r_cat, r_hi, alpha, tm):
    n, d = x.shape
    main_w = y_main.shape[1]
    mem_w = y_mem.shape[1]
    nt = n // tm
    chunks = d // LANES
    cur = lambda i: (jnp.minimum(i, nt - 1), 0)
    prev = lambda i: (jnp.maximum(i - 1, 0), 0)
    return pl.pallas_call(
        functools.partial(_out_ln_kernel, alpha=alpha),
        grid=(nt + 1,),
        in_specs=[pl.BlockSpec((tm, main_w), cur),
                  pl.BlockSpec((tm, mem_w), cur),
                  pl.BlockSpec((tm, d), prev),
                  pl.BlockSpec((None, main_w, d), lambda i: (layer, 0, 0)),
                  pl.BlockSpec((None, mem_w, d), lambda i: (layer, main_w // mem_w, 0)),
                  pl.BlockSpec((1, d), lambda i: (0, 0)),
                  pl.BlockSpec((1, d), lambda i: (0, 0)),
                  pl.BlockSpec((d, 2 * LANES), lambda i: (0, 0)),
                  pl.BlockSpec((d, LANES), lambda i: (0, 0))],
        out_specs=[pl.BlockSpec((tm * chunks, LANES), prev),
                   pl.BlockSpec((ROUTE_ROWS, tm), lambda i: (0, jnp.maximum(i - 1, 0)))],
        out_shape=[jax.ShapeDtypeStruct((n * chunks, LANES), F32),
                   jax.ShapeDtypeStruct((ROUTE_ROWS, n), F32)],
        scratch_shapes=[pltpu.VMEM((tm, d), F32)],
        compiler_params=_params(("arbitrary",)),
        name="out_ln_route",
    )(y_main, y_mem, x, w_out_bf, w_out_bf, gamma.reshape(1, d), beta.reshape(1, d), r_cat, r_hi)


def _route_kernel(lt_ref, rb_ref, e1_ref, e2_ref, g1_ref, g2_ref):
    lt = lt_ref[...] + rb_ref[...]
    ng, ne = N_GROUPS, EXP_PER_GROUP
    lg = [lt[g:g + 1, :] for g in range(ng)]
    m = lg[0]
    gi = jnp.zeros(m.shape, I32)
    for g in range(1, ng):
        better = lg[g] > m
        gi = jnp.where(better, g, gi)
        m = jnp.where(better, lg[g], m)
    den = jnp.exp(lg[0] - m)
    for g in range(1, ng):
        den = den + jnp.exp(lg[g] - m)
    p_sel = 1.0 / den
    le = []
    for e in range(ne):
        v = lt[ng + e:ng + e + 1, :]
        for g in range(1, ng):
            r = ng + g * ne + e
            v = jnp.where(gi == g, lt[r:r + 1, :], v)
        le.append(v)
    v1 = le[0]
    i1 = jnp.zeros(m.shape, I32)
    for e in range(1, ne):
        better = le[e] > v1
        i1 = jnp.where(better, e, i1)
        v1 = jnp.where(better, le[e], v1)
    v2 = jnp.full(m.shape, -jnp.inf, F32)
    i2 = jnp.zeros(m.shape, I32)
    for e in range(ne):
        better = (i1 != e) & (le[e] > v2)
        i2 = jnp.where(better, e, i2)
        v2 = jnp.where(better, le[e], v2)
    t = jnp.exp(v2 - v1)
    w_a = 1.0 / (1.0 + t)
    w_b = t / (1.0 + t)
    e1_ref[...] = gi * ne + i1
    e2_ref[...] = gi * ne + i2
    g1_ref[...] = p_sel * w_a
    g2_ref[...] = p_sel * w_b


def _route(lt, rbias):
    n = lt.shape[1]
    shp = lambda dt: jax.ShapeDtypeStruct((1, n), dt)
    return pl.pallas_call(
        _route_kernel,
        out_shape=[shp(I32), shp(I32), shp(F32), shp(F32)],
        compiler_params=pltpu.CompilerParams(vmem_limit_bytes=VMEM_LIMIT),
        name="route",
    )(lt, rbias)


def _moe_plan(e1, e2, tm, n_tiles):
    n = e1.shape[0]
    e = jnp.concatenate([e1, e2])
    oh = (e[:, None] == jnp.arange(N_EXPERTS, dtype=I32)[None, :]).astype(I32)
    csum = jnp.cumsum(oh, axis=0)
    rank = jnp.sum((csum - oh) * oh, axis=1)
    counts = csum[-1]
    tiles = (counts + tm - 1) // tm
    tile_end = jnp.cumsum(tiles)
    starts = (tile_end - tiles) * tm
    dest = jnp.sum(oh * starts[None, :], axis=1) + rank
    n_used = tile_end[-1]
    tidx = jnp.arange(n_tiles, dtype=I32)
    te = jnp.sum((tidx[:, None] >= tile_end[None, :]).astype(I32), axis=1)
    last_e = jnp.max(jnp.where(tiles > 0, jnp.arange(N_EXPERTS, dtype=I32), 0))
    te = jnp.where(tidx < n_used, te, last_e).astype(I32)
    rows = n_tiles * tm
    pair_of_row = jnp.full((rows,), -1, I32).at[dest].set(jnp.arange(2 * n, dtype=I32))
    has_pair = pair_of_row >= 0
    filler = jnp.logical_and(~has_pair, jnp.arange(rows, dtype=I32) < n_used * tm)
    spill = 2 * n + tm + jnp.cumsum(filler.astype(I32)) - 1
    src = jnp.where(has_pair, pair_of_row % n, 0).astype(I32)
    dst_rows = jnp.where(has_pair, pair_of_row, jnp.where(filler, spill, 2 * n))
    dst = jnp.concatenate([2 * n + jnp.arange(tm, dtype=I32), dst_rows]).astype(I32)
    return src, dst, te, n_used.reshape(1).astype(I32)


def _moe_kernel(te_ref, src_ref, dst_ref, nu_ref, x_hbm, wg_ref, wu_ref, wd_ref, y_hbm,
                xbuf, ybuf, gsem, ssem, wg_s, wu_s, wd_s, *, tm, n_pairs, n_spill_tiles):
    j = pl.program_id(0)
    n_used = nu_ref[0]
    slot = j % 3
    ahead = (j + 2) % 3
    chunks = wg_s.shape[0] // LANES
    tile_rows = tm * chunks

    def token_rows(token):
        return pl.ds(pl.multiple_of(token * chunks, chunks), chunks)

    def slot_rows(s):
        return pl.ds(pl.multiple_of(s * tile_rows, tile_rows), tile_rows)

    def start_gather(tile, to_slot):
        for r in range(tm):
            pltpu.make_async_copy(x_hbm.at[token_rows(src_ref[tile * tm + r])],
                                  xbuf.at[token_rows(to_slot * tm + r)], gsem.at[to_slot]).start()

    def wait_gather(of_slot):
        pltpu.make_async_copy(x_hbm.at[pl.ds(0, tile_rows)], xbuf.at[slot_rows(of_slot)],
                              gsem.at[of_slot]).wait()

    def start_scatter(tile, from_slot):
        for r in range(tm):
            pltpu.make_async_copy(ybuf.at[token_rows(from_slot * tm + r)],
                                  y_hbm.at[token_rows(dst_ref[(tile + 1) * tm + r])],
                                  ssem.at[from_slot]).start()

    def wait_scatter(of_slot):
        pltpu.make_async_copy(ybuf.at[slot_rows(of_slot)], y_hbm.at[pl.ds(0, tile_rows)],
                              ssem.at[of_slot]).wait()

    @pl.when(j == 0)
    def _():
        start_gather(0, 0)
        start_gather(1, 1)
        ybuf[pl.ds(2 * tile_rows, tile_rows), :] = jnp.zeros((tile_rows, LANES), F32)
        fills = [pltpu.make_async_copy(ybuf.at[pl.ds(2 * tile_rows, tile_rows)],
                                       y_hbm.at[pl.ds((n_pairs + c * tm) * chunks, tile_rows)],
                                       ssem.at[2]) for c in range(1, n_spill_tiles)]
        for f in fills:
            f.start()
        for f in fills:
            f.wait()

    e = te_ref[j]
    e_prev = te_ref[jnp.maximum(j - 1, 0)]

    @pl.when((j == 0) | (e != e_prev))
    def _():
        wg_s[...] = wg_ref[0, 0].astype(BF)
        wu_s[...] = wu_ref[0, 0].astype(BF)
        wd_s[...] = wd_ref[0, 0].astype(BF)

    @pl.when((j >= 2) & (j <= n_used + 2))
    def _():
        wait_scatter(slot)

    @pl.when(j < n_used)
    def _():
        wait_gather(slot)
        start_gather(j + 2, ahead)
        start_scatter(j - 1, ahead)
        base = slot * tile_rows
        x = jnp.concatenate([xbuf[pl.ds(base + c, tm, stride=chunks), :] for c in range(chunks)],
                            axis=1).astype(BF)
        g = _dot(x, wg_s[...])
        u = _dot(x, wu_s[...])
        h = (g * jax.nn.sigmoid(g) * u).astype(BF)
        y = _dot(h, wd_s[...])
        for c in range(chunks):
            ybuf[pl.ds(base + c, tm, stride=chunks), :] = y[:, c * LANES:(c + 1) * LANES]

    @pl.when(j == n_used)
    def _():
        wait_gather(slot)
        start_scatter(j - 1, ahead)

    @pl.when(j == n_used + 1)
    def _():
        wait_gather(slot)


def _moe_experts(x1, src, dst, te, n_used, w_gate, w_up, w_down, layer, tm, n_tiles):
    d, ff = w_gate.shape[-2:]
    chunks = d // LANES
    n = x1.shape[0] // chunks
    n_spill_tiles = 1 + -(-N_EXPERTS * (tm - 1) // tm)
    grid_spec = pltpu.PrefetchScalarGridSpec(
        num_scalar_prefetch=4,
        grid=(n_tiles,),
        in_specs=[pl.BlockSpec(memory_space=pl.ANY),
                  pl.BlockSpec((1, 1, d, ff), lambda j, te, *_: (layer, te[j], 0, 0)),
                  pl.BlockSpec((1, 1, d, ff), lambda j, te, *_: (layer, te[j], 0, 0)),
                  pl.BlockSpec((1, 1, ff, d), lambda j, te, *_: (layer, te[j], 0, 0))],
        out_specs=pl.BlockSpec(memory_space=pl.ANY),
        scratch_shapes=[pltpu.VMEM((3 * tm * chunks, LANES), F32),
                        pltpu.VMEM((3 * tm * chunks, LANES), F32),
                        pltpu.SemaphoreType.DMA((3,)),
                        pltpu.SemaphoreType.DMA((3,)),
                        pltpu.VMEM((d, ff), BF),
                        pltpu.VMEM((d, ff), BF),
                        pltpu.VMEM((ff, d), BF)],
    )
    return pl.pallas_call(
        functools.partial(_moe_kernel, tm=tm, n_pairs=2 * n, n_spill_tiles=n_spill_tiles),
        grid_spec=grid_spec,
        out_shape=jax.ShapeDtypeStruct(((2 * n + n_spill_tiles * tm) * chunks, LANES), F32),
        compiler_params=_params(("arbitrary",)),
        name="moe_experts",
    )(te, src, dst, n_used, x1, w_gate, w_up, w_down)


def _combine_kernel(x1_ref, ya_ref, yb_ref, g1_ref, g2_ref, gam_ref, bet_ref,
                    x2_ref, x2b_ref, *, alpha):
    tm, d = x2_ref.shape
    chunks = d // LANES
    g1 = g1_ref[...]
    g2 = g2_ref[...]
    cols = []
    for c in range(chunks):
        rows = pl.ds(c, tm, stride=chunks)
        cols.append(alpha * x1_ref[rows, :] + g1 * ya_ref[rows, :] + g2 * yb_ref[rows, :])
    x2 = _layer_norm(jnp.concatenate(cols, axis=1), gam_ref[...], bet_ref[...])
    x2_ref[...] = x2
    x2b_ref[...] = x2.astype(BF)


def _combine_ln(x1, y_pairs, g1, g2, gamma, beta, alpha, tm):
    d = gamma.shape[0]
    chunks = d // LANES
    n = x1.shape[0] // chunks
    nb = n // tm
    return pl.pallas_call(
        functools.partial(_combine_kernel, alpha=alpha),
        grid=(nb,),
        in_specs=[pl.BlockSpec((tm * chunks, LANES), lambda i: (i, 0)),
                  pl.BlockSpec((tm * chunks, LANES), lambda i: (i, 0)),
                  pl.BlockSpec((tm * chunks, LANES), lambda i: (i + nb, 0)),
                  pl.BlockSpec((tm, 1), lambda i: (i, 0)),
                  pl.BlockSpec((tm, 1), lambda i: (i, 0)),
                  pl.BlockSpec((1, d), lambda i: (0, 0)),
                  pl.BlockSpec((1, d), lambda i: (0, 0))],
        out_specs=[pl.BlockSpec((tm, d), lambda i: (i, 0)),
                   pl.BlockSpec((tm, d), lambda i: (i, 0))],
        out_shape=[jax.ShapeDtypeStruct((n, d), F32), jax.ShapeDtypeStruct((n, d), BF)],
        compiler_params=_params(("parallel",)),
        name="combine_ln",
    )(x1, y_pairs, y_pairs, g1.reshape(n, 1), g2.reshape(n, 1), gamma.reshape(1, d),
      beta.reshape(1, d))


def _hier_moe_ln(x1, lt, rbias, w_gate, w_up, w_down, layer, gamma, beta, alpha, tm_moe, tm_tok):
    n = lt.shape[1]
    e1, e2, g1, g2 = _route(lt, rbias)
    n_tiles = (2 * n + N_EXPERTS * (tm_moe - 1)) // tm_moe + 3
    src, dst, te, n_used = _moe_plan(e1[0], e2[0], tm_moe, n_tiles)
    y_pairs = _moe_experts(x1, src, dst, te, n_used, w_gate, w_up, w_down, layer, tm_moe, n_tiles)
    return _combine_ln(x1, y_pairs, g1[0], g2[0], gamma, beta, alpha, tm_tok)


def _trunk(x3, pool_prev, mem_k, mem_v, sb_past, p, *, decode):
    b, t, d = x3.shape
    depth = p["w_in"].shape[0]
    n_a = p["w_pool"].shape[0]
    main_w = p["w_sb_kv"].shape[-1] // 2
    mem_w = d - main_w
    n_heads = main_w // HEAD_DIM
    alpha = float((2 * depth) ** 0.25)
    n = b * t
    tm = min(512, n)
    tm_c = min(256, n)
    tm_moe = 256 if n >= 2048 else 16
    pos0 = 0 if not decode else sb_past[2].shape[1] * sb_past[0].shape[1]

    x = x3.reshape(n, d)
    xb = x
    new_pool = []
    k_bf = v_bf = k_f = v_f = None
    for l in range(depth):
        (proj,) = _matmul(xb, p["w_in"], l, [(0, d, F32)], tm, "proj_in")
        proj3 = proj.reshape(b, t, d)
        main = proj3[..., :main_w]
        if l < n_a:
            if decode:
                u_tm = jnp.transpose(main, (1, 0, 2))
                prev_tm = jnp.transpose(pool_prev[l], (1, 0, 2))
                y_tm = _pool_mix_decode(u_tm, prev_tm, p["w_pool"][l], p["pool_scale"][l], pos0)
                y_main = jnp.transpose(y_tm, (1, 0, 2))
            else:
                y_main = _pool_mix(proj3, pool_prev[l], p["w_pool"][l], p["pool_scale"][l],
                                   main_w, min(256, t), pos0)
            if t >= POOL_STATE:
                new_pool.append(main[:, t - POOL_STATE:, :])
            else:
                new_pool.append(jnp.concatenate([pool_prev[l][:, t:, :], main], axis=1))
        else:
            bias = p["sb_bias"][l - n_a]
            if decode:
                y_main = _stick_breaking_decode(main, k_f.reshape(b, t, main_w),
                                                v_f.reshape(b, t, main_w),
                                                sb_past[0], sb_past[1], sb_past[2], bias, pages=8)
            else:
                y_main = _stick_breaking_prompt(proj3, k_bf.reshape(b, t, main_w),
                                                v_bf.reshape(b, t, main_w), bias, n_heads, 256, 4)
        if decode:
            t_pad = 8
            proj_pad = jnp.pad(proj3, ((0, 0), (0, t_pad - t), (0, 0)))
            y_mem = _mem_attend(proj_pad, mem_k[l], mem_v[l], mem_w, t_pad)[:, :t]
        else:
            y_mem = _mem_attend(proj3, mem_k[l], mem_v[l], mem_w, min(512, t))
        x1, lt = _out_ln_route(y_main.reshape(n, main_w), y_mem.reshape(n, mem_w), x,
                               p["w_out"], l, p["ln1_g"][l], p["ln1_b"][l],
                               p["r_cat"][l], p["r_hi"][l], alpha, tm_c)
        x, xb = _hier_moe_ln(x1, lt, p["r_bias"][l], p["w_gate"], p["w_up"], p["w_down"], l,
                             p["ln2_g"][l], p["ln2_b"][l], alpha, tm_moe, tm)
        if l == n_a - 1:
            if decode:
                k_f, v_f = _matmul(xb, p["w_sb_kv"], 0, [(0, main_w, F32), (main_w, main_w, F32)],
                                   n, "proj_sb_kv")
                k_out = k_f.reshape(b, t, n_heads, HEAD_DIM)
                v_out = v_f.reshape(b, t, n_heads, HEAD_DIM)
            else:
                k_hm, v_hm, k_bf, v_bf = _kv_project(xb, p["w_sb_kv"][0], b, t, 256)
                k_out = jnp.transpose(k_hm, (0, 2, 1, 3))
                v_out = jnp.transpose(v_hm, (0, 2, 1, 3))
    return x.reshape(b, t, d), jnp.stack(new_pool, axis=0), k_out, v_out


def kernel(x_prompt, x_sample, state_pool, cache_sb_k, cache_sb_v, cache_mem_k, cache_mem_v,
           page_table, mem_prompt, w_in, w_out, w_pool_grp, pool_scale, w_mem_k, w_mem_v,
           ln1_g, ln1_b, ln2_g, ln2_b, w_route_grp, b_route_grp, w_route_exp, b_route_exp,
           w_gate, w_up, w_down, w_sb_k, w_sb_v, sb_bias):
    depth, d, _ = w_in.shape
    bp, _, _ = x_prompt.shape
    bs, ts, _ = x_sample.shape
    n_a = w_pool_grp.shape[0]
    main_w = w_sb_k.shape[1]
    mem_w = w_mem_k.shape[2]
    n_mem = mem_prompt.shape[1]
    n_heads = main_w // HEAD_DIM
    mem_heads = mem_w // HEAD_DIM

    wr = jnp.concatenate([w_route_grp, w_route_exp.reshape(depth, d, N_EXPERTS)], axis=-1)
    wr = jnp.pad(wr, ((0, 0), (0, 0), (0, LANES - N_GROUPS - N_EXPERTS)))
    r_hi = wr.astype(BF)
    r_lo = (wr - r_hi.astype(F32)).astype(BF)
    r_cat = jnp.concatenate([r_hi, r_lo], axis=-1)
    r_bias = jnp.pad(jnp.concatenate([b_route_grp, b_route_exp.reshape(depth, N_EXPERTS)], axis=-1),
                     ((0, 0), (0, ROUTE_ROWS - N_GROUPS - N_EXPERTS))).reshape(depth, ROUTE_ROWS, 1)
    p = dict(
        w_in=w_in.astype(BF), w_out=w_out.astype(BF), w_pool=w_pool_grp.astype(BF),
        pool_scale=pool_scale, w_sb_kv=jnp.concatenate([w_sb_k, w_sb_v], axis=1).astype(BF)[None],
        ln1_g=ln1_g, ln1_b=ln1_b, ln2_g=ln2_g, ln2_b=ln2_b, r_hi=r_hi, r_cat=r_cat, r_bias=r_bias,
        w_gate=w_gate, w_up=w_up, w_down=w_down, sb_bias=sb_bias)

    w_mem = jnp.concatenate([jnp.transpose(w_mem_k, (1, 0, 2)).reshape(d, depth * mem_w),
                             jnp.transpose(w_mem_v, (1, 0, 2)).reshape(d, depth * mem_w)],
                            axis=1).astype(BF)
    mem_kv = _mem_project(mem_prompt.reshape(bp * n_mem, d).astype(BF), w_mem, mem_w)
    mem_kv = mem_kv.reshape(2, depth, bp, n_mem, mem_w)
    pool0 = jnp.zeros((n_a, bp, POOL_STATE, main_w), x_prompt.dtype)
    y_p, pool_p, k_p, v_p = _trunk(x_prompt, pool0, mem_kv[0], mem_kv[1], None, p, decode=False)

    sb_past = (cache_sb_k, cache_sb_v, page_table)
    y_s, pool_s, k_s, v_s = _trunk(
        x_sample, state_pool, cache_mem_k.reshape(depth, bs, n_mem, mem_w),
        cache_mem_v.reshape(depth, bs, n_mem, mem_w), sb_past, p, decode=True)

    return (y_p, y_s, pool_p, pool_s, k_p, v_p, k_s, v_s,
            mem_kv[0].reshape(depth, bp, n_mem, mem_heads, HEAD_DIM),
            mem_kv[1].reshape(depth, bp, n_mem, mem_heads, HEAD_DIM))
```

```python
import functools

import jax
import jax.numpy as jnp
import numpy as np
from jax import lax
from jax.experimental import pallas as pl
from jax.experimental.pallas import tpu as pltpu

F32 = jnp.float32
BF = jnp.bfloat16
I32 = jnp.int32

HEAD_DIM = 128
LANES = 128
POOL_WINDOWS = (2, 4, 8, 16)
POOL_STATE = max(POOL_WINDOWS) - 1
N_GROUPS = 4
EXP_PER_GROUP = 4
N_EXPERTS = N_GROUPS * EXP_PER_GROUP
LN_EPS = 1e-5
ROUTE_ROWS = 32
VMEM_LIMIT = 56 * 2**20

_NT = (((1,), (1,)), ((), ()))


def _params(sem):
    return pltpu.CompilerParams(dimension_semantics=sem, vmem_limit_bytes=VMEM_LIMIT)


def _dot(a, b):
    return jnp.dot(a, b, preferred_element_type=F32)


def _layer_norm(v, g, b):
    mu = jnp.mean(v, axis=-1, keepdims=True)
    c = v - mu
    var = jnp.mean(c * c, axis=-1, keepdims=True)
    return c * lax.rsqrt(var + LN_EPS) * g + b


def _softplus(z):
    return jnp.maximum(z, 0.0) + jnp.log(1.0 + jnp.exp(jnp.minimum(z, -z)))


def _split_bf16(x):
    hi = x.astype(BF)
    lo = (x - hi.astype(F32)).astype(BF)
    return hi, lo


def _mm_kernel(x_ref, w_ref, *o_refs, offsets):
    acc = _dot(x_ref[...].astype(BF), w_ref[...])
    for o_ref, off in zip(o_refs, offsets):
        o_ref[...] = acc[:, off:off + o_ref.shape[-1]].astype(o_ref.dtype)


def _matmul(x, w, layer, outs, tm, name):
    m, k = x.shape
    n = w.shape[-1]
    return pl.pallas_call(
        functools.partial(_mm_kernel, offsets=tuple(o for o, _, _ in outs)),
        grid=(m // tm,),
        in_specs=[pl.BlockSpec((tm, k), lambda i: (i, 0)),
                  pl.BlockSpec((None, k, n), lambda i: (layer, 0, 0))],
        out_specs=[pl.BlockSpec((tm, c), lambda i: (i, 0)) for _, c, _ in outs],
        out_shape=[jax.ShapeDtypeStruct((m, c), dt) for _, c, dt in outs],
        compiler_params=_params(("parallel",)),
        name=name,
    )(x, w)


def _kv_kernel(x_ref, w_ref, kf_ref, vf_ref, kb_ref, vb_ref):
    n_heads = kf_ref.shape[1]
    main_w = n_heads * HEAD_DIM
    acc = _dot(x_ref[...], w_ref[...])
    for h in range(n_heads):
        kf_ref[0, h] = acc[:, h * HEAD_DIM:(h + 1) * HEAD_DIM]
        vf_ref[0, h] = acc[:, main_w + h * HEAD_DIM:main_w + (h + 1) * HEAD_DIM]
    kb_ref[...] = acc[:, :main_w].astype(BF)
    vb_ref[...] = acc[:, main_w:].astype(BF)


def _kv_project(xb, w_kv, b, t, tm):
    n, d = xb.shape
    main_w = w_kv.shape[1] // 2
    n_heads = main_w // HEAD_DIM
    nt = t // tm
    head_major = pl.BlockSpec((1, n_heads, tm, HEAD_DIM), lambda i: (i // nt, 0, i % nt, 0))
    return pl.pallas_call(
        _kv_kernel,
        grid=(n // tm,),
        in_specs=[pl.BlockSpec((tm, d), lambda i: (i, 0)),
                  pl.BlockSpec((d, 2 * main_w), lambda i: (0, 0))],
        out_specs=[head_major, head_major,
                   pl.BlockSpec((tm, main_w), lambda i: (i, 0)),
                   pl.BlockSpec((tm, main_w), lambda i: (i, 0))],
        out_shape=[jax.ShapeDtypeStruct((b, n_heads, t, HEAD_DIM), F32)] * 2
                  + [jax.ShapeDtypeStruct((n, main_w), BF)] * 2,
        compiler_params=_params(("parallel",)),
        name="proj_sb_kv",
    )(xb, w_kv)


def _memproj_kernel(x_ref, w_ref, o_ref):
    o_ref[0] = _dot(x_ref[...], w_ref[...])


def _mem_project(mem_bf, w_cat, width):
    m, d = mem_bf.shape
    nj = w_cat.shape[1] // width
    return pl.pallas_call(
        _memproj_kernel,
        grid=(nj,),
        in_specs=[pl.BlockSpec((m, d), lambda j: (0, 0)),
                  pl.BlockSpec((d, width), lambda j: (0, j))],
        out_specs=pl.BlockSpec((1, m, width), lambda j: (j, 0, 0)),
        out_shape=jax.ShapeDtypeStruct((nj, m, width), F32),
        compiler_params=_params(("parallel",)),
        name="mem_project",
    )(mem_bf, w_cat)


def _pool_kernel(u_ref, prev_ref, w_ref, sc_ref, y_ref, buf, *, tt, nt, pos0):
    t = pl.program_id(1)
    p = POOL_STATE
    cg = w_ref.shape[-1]

    @pl.when(t == 0)
    def _():
        buf[1:1 + p, :] = prev_ref[0]

    u = u_ref[0]
    buf[p + 1:p + 1 + tt, :] = u
    pos = lax.broadcasted_iota(I32, (tt, 1), 0) + (t * tt + pos0)
    for g, w in enumerate(POOL_WINDOWS):
        c0 = g * cg
        ug = u[:, c0:c0 + cg]
        s = ug
        for k in range(1, w):
            s = s + buf[p + 1 - k:p + 1 - k + tt, c0:c0 + cg]
        cnt = jnp.minimum(w, pos + 1).astype(F32)
        d = (s / cnt - ug).astype(BF)
        yg = _dot(d, w_ref[g]) * sc_ref[:, c0:c0 + cg]
        y_ref[0, :, c0:c0 + cg] = yg.astype(y_ref.dtype)
    if nt > 1:
        buf[1:1 + p, :] = buf[tt + 1:tt + 1 + p, :]


def _pool_mix(proj, prev, w_grp_bf, scale, main_w, tt, pos0):
    b, t, _ = proj.shape
    nt = t // tt
    cg = main_w // len(POOL_WINDOWS)
    return pl.pallas_call(
        functools.partial(_pool_kernel, tt=tt, nt=nt, pos0=pos0),
        grid=(b, nt),
        in_specs=[pl.BlockSpec((1, tt, main_w), lambda i, j: (i, j, 0)),
                  pl.BlockSpec((1, POOL_STATE, main_w), lambda i, j: (i, 0, 0)),
                  pl.BlockSpec((len(POOL_WINDOWS), cg, cg), lambda i, j: (0, 0, 0)),
                  pl.BlockSpec((1, main_w), lambda i, j: (0, 0))],
        out_specs=pl.BlockSpec((1, tt, main_w), lambda i, j: (i, j, 0)),
        out_shape=jax.ShapeDtypeStruct((b, t, main_w), BF),
        scratch_shapes=[pltpu.VMEM((POOL_STATE + 1 + tt, main_w), F32)],
        compiler_params=_params(("parallel", "arbitrary")),
        name="pool_mix",
    )(proj, prev, w_grp_bf, scale.reshape(1, main_w))


def _pool_dec_kernel(u_ref, prev_ref, w_ref, sc_ref, y_ref, *, pos0):
    p = POOL_STATE
    nt = u_ref.shape[0]
    cg = w_ref.shape[-1]
    cat = [prev_ref[k] for k in range(p)] + [u_ref[i] for i in range(nt)]
    for g, w in enumerate(POOL_WINDOWS):
        c0 = g * cg
        ds = []
        for i in range(nt):
            s = cat[p + i][:, c0:c0 + cg]
            for k in range(1, w):
                s = s + cat[p + i - k][:, c0:c0 + cg]
            cnt = float(min(w, pos0 + i + 1))
            ds.append(s / cnt - cat[p + i][:, c0:c0 + cg])
        d = jnp.concatenate(ds, axis=0).astype(BF)
        yg = _dot(d, w_ref[g]) * sc_ref[:, c0:c0 + cg]
        nb = ds[0].shape[0]
        for i in range(nt):
            y_ref[i, :, c0:c0 + cg] = yg[i * nb:(i + 1) * nb].astype(y_ref.dtype)


def _pool_mix_decode(u_tm, prev_tm, w_grp_bf, scale, pos0):
    nt, b, c = u_tm.shape
    return pl.pallas_call(
        functools.partial(_pool_dec_kernel, pos0=pos0),
        out_shape=jax.ShapeDtypeStruct((nt, b, c), BF),
        compiler_params=pltpu.CompilerParams(vmem_limit_bytes=VMEM_LIMIT),
        name="pool_mix_decode",
    )(u_tm, prev_tm, w_grp_bf, scale.reshape(1, c))


def _mem_kernel(q_ref, k_ref, v_ref, o_ref, *, cdt):
    nh = q_ref.shape[-1] // HEAD_DIM
    scale = HEAD_DIM ** -0.5
    for h in range(nh):
        sl = slice(h * HEAD_DIM, (h + 1) * HEAD_DIM)
        q = q_ref[0, :, sl].astype(cdt)
        k = k_ref[0, :, sl].astype(cdt)
        v = v_ref[0, :, sl].astype(cdt)
        s = lax.dot_general(q, k, _NT, preferred_element_type=F32) * scale
        e = jnp.exp(s - jnp.max(s, axis=-1, keepdims=True))
        den = jnp.sum(e, axis=-1, keepdims=True)
        o = _dot(e.astype(cdt), v) / den
        o_ref[0, :, sl] = o.astype(o_ref.dtype)


def _mem_attend(proj, mk, mv, mem_w, tt):
    b, t, mix_w = proj.shape
    n_mem = mk.shape[1]
    qblk = (mix_w - mem_w) // mem_w
    cdt = BF if tt % 16 == 0 else F32
    return pl.pallas_call(
        functools.partial(_mem_kernel, cdt=cdt),
        grid=(b, t // tt),
        in_specs=[pl.BlockSpec((1, tt, mem_w), lambda i, j: (i, j, qblk)),
                  pl.BlockSpec((1, n_mem, mem_w), lambda i, j: (i, 0, 0)),
                  pl.BlockSpec((1, n_mem, mem_w), lambda i, j: (i, 0, 0))],
        out_specs=pl.BlockSpec((1, tt, mem_w), lambda i, j: (i, j, 0)),
        out_shape=jax.ShapeDtypeStruct((b, t, mem_w), BF),
        compiler_params=_params(("parallel", "parallel")),
        name="mem_attend",
    )(proj, mk, mv)


def _sb_kernel(q_ref, k_ref, v_ref, bias_ref, o_ref, *, tq, hp):
    qi = pl.program_id(2)
    inv_sqrt = 1.0 / np.sqrt(np.float32(HEAD_DIM))
    jj = lax.broadcasted_iota(I32, (tq, tq), 0)
    ss = lax.broadcasted_iota(I32, (tq, tq), 1)
    later = (jj > ss).astype(BF)
    later2 = jnp.concatenate([later, later], axis=0)
    causal = ss < jj
    heads = [slice(h * HEAD_DIM, (h + 1) * HEAD_DIM) for h in range(hp)]
    qs = [(q_ref[0, :, sl] * inv_sqrt).astype(BF) for sl in heads]
    biases = [bias_ref[0, h:h + 1, :] for h in range(hp)]

    def blocks(start, state, mask):
        rng = range(hp)
        zs = [lax.dot_general(qs[h], k_ref[0, pl.ds(start, tq), heads[h]], _NT,
                              preferred_element_type=F32) + biases[h] for h in rng]
        sps = [_softplus(z) for z in zs]
        if mask is not None:
            sps = [jnp.where(mask, sp, 0.0) for sp in sps]
        splits = [_split_bf16(sp) for sp in sps]
        later_sums = [_dot(jnp.concatenate([hi, lo], axis=1), later2) for hi, lo in splits]
        ws = [jnp.exp(zs[h] - sps[h] - later_sums[h] + state[h][0]) for h in rng]
        if mask is not None:
            ws = [jnp.where(mask, a, 0.0) for a in ws]
        accs = [state[h][1] + _dot(ws[h].astype(BF), v_ref[0, pl.ds(start, tq), heads[h]])
                for h in rng]
        tails = [state[h][0] - jnp.sum(sps[h], axis=1, keepdims=True) for h in rng]
        return tuple(zip(tails, accs))

    diag = pl.multiple_of(qi * tq, tq)
    zero = (jnp.zeros((tq, 1), F32), jnp.zeros((tq, HEAD_DIM), F32))
    state = blocks(diag, (zero,) * hp, causal)

    def body(n, carry):
        return blocks(pl.multiple_of((qi - 1 - n) * tq, tq), carry, None)

    state = lax.fori_loop(0, qi, body, state)
    for h in range(hp):
        o_ref[0, :, heads[h]] = state[h][1].astype(o_ref.dtype)


def _stick_breaking_prompt(proj, k_bf, v_bf, bias, n_heads, tq, hp):
    b, t, _ = proj.shape
    main_w = n_heads * HEAD_DIM
    w = hp * HEAD_DIM
    bias_b = jnp.broadcast_to(bias.astype(F32).reshape(n_heads // hp, hp, 1),
                              (n_heads // hp, hp, tq))
    return pl.pallas_call(
        functools.partial(_sb_kernel, tq=tq, hp=hp),
        grid=(b, n_heads // hp, t // tq),
        in_specs=[pl.BlockSpec((1, tq, w), lambda i, h, j: (i, j, h)),
                  pl.BlockSpec((1, t, w), lambda i, h, j: (i, 0, h)),
                  pl.BlockSpec((1, t, w), lambda i, h, j: (i, 0, h)),
                  pl.BlockSpec((1, hp, tq), lambda i, h, j: (h, 0, 0))],
        out_specs=pl.BlockSpec((1, tq, w), lambda i, h, j: (i, j, h)),
        out_shape=jax.ShapeDtypeStruct((b, t, main_w), BF),
        compiler_params=_params(("parallel", "parallel", "arbitrary")),
        name="stick_breaking_prompt",
    )(proj, k_bf, v_bf, bias_b)


def _sb_dec_kernel(pt_ref, qt_ref, bias_ref, kn_ref, vn_ref, *rest, pages, n_q, n_new):
    k_refs = rest[:pages]
    v_refs = rest[pages:2 * pages]
    o_ref, tail_ref, acc_ref = rest[2 * pages:]
    j = pl.program_id(1)
    inv_sqrt = 1.0 / np.sqrt(np.float32(HEAD_DIM))
    qt = qt_ref[0]
    bias = bias_ref[...]
    n_cols = acc_ref.shape[0]
    nk = kn_ref.shape[1]
    n_heads = k_refs[0].shape[1]
    ss = lax.broadcasted_iota(I32, (nk, nk), 0)
    jj = lax.broadcasted_iota(I32, (nk, nk), 1)
    later = (jj > ss).astype(BF)

    def heads_cat(ref):
        return jnp.concatenate([ref[0, h] for h in range(n_heads)], axis=1).astype(BF)

    def scores(kcat, mask):
        z = _dot(kcat, qt) * inv_sqrt + bias
        sp = _softplus(z)
        if mask is not None:
            sp = jnp.where(mask, sp, 0.0)
        hi, lo = _split_bf16(sp)
        both = _dot(later, jnp.concatenate([hi, lo], axis=1))
        later_sum = both[:, :128] + both[:, 128:]
        return z, sp, later_sum, jnp.sum(sp, axis=0, keepdims=True)

    def weights(z, sp, later_sum, tail, mask):
        a = jnp.exp(z - sp - later_sum + tail)
        if mask is not None:
            a = jnp.where(mask, a, 0.0)
        return a.T[:n_cols].astype(BF)

    @pl.when(j == 0)
    def _():
        r = lax.broadcasted_iota(I32, (nk, 128), 0)
        c = lax.broadcasted_iota(I32, (nk, 128), 1)
        mask = (r < c % n_q) & (r < n_new)
        z, sp, later_sum, total = scores(kn_ref[0].astype(BF), mask)
        a_t = weights(z, sp, later_sum, jnp.zeros_like(total), mask)
        acc_ref[...] = _dot(a_t, vn_ref[0].astype(BF))
        tail_ref[...] = -total

    parts = [scores(heads_cat(k_refs[g]), None) for g in range(pages)]
    tail = tail_ref[...]
    a_ts = []
    for z, sp, later_sum, total in parts:
        a_ts.append(weights(z, sp, later_sum, tail, None))
        tail = tail - total
    v_all = jnp.concatenate([heads_cat(v_refs[g]) for g in range(pages)], axis=0)
    acc_ref[...] += _dot(jnp.concatenate(a_ts, axis=1), v_all)
    tail_ref[...] = tail

    @pl.when(j == pl.num_programs(1) - 1)
    def _():
        for h in range(o_ref.shape[-1] // HEAD_DIM):
            sl = slice(h * HEAD_DIM, (h + 1) * HEAD_DIM)
            o_ref[0, :, sl] = acc_ref[h * n_q:(h + 1) * n_q, sl].astype(o_ref.dtype)


def _stick_breaking_decode(q, k_new, v_new, cache_k, cache_v, page_table, bias, pages):
    b, t, main_w = q.shape
    n_pages = page_table.shape[1]
    _, page, n_heads, _ = cache_k.shape
    n_cols = n_heads * t
    qh = q.reshape(b, t, n_heads, HEAD_DIM).astype(F32)
    eye = jnp.eye(n_heads, dtype=F32)
    qt = jnp.einsum("bihd,hg->bhdgi", qh, eye).reshape(b, main_w, n_cols)
    qt = jnp.pad(qt, ((0, 0), (0, 0), (0, 128 - n_cols))).astype(BF)
    bias_c = jnp.pad(jnp.repeat(bias.astype(F32), t), (0, 128 - n_cols)).reshape(1, 128)
    new_rows = page
    k_pad = jnp.pad(k_new, ((0, 0), (0, new_rows - t), (0, 0)))
    v_pad = jnp.pad(v_new, ((0, 0), (0, new_rows - t), (0, 0)))

    def page_map(g):
        return lambda i, j, pt: (pt[i, n_pages - 1 - (j * pages + g)], 0, 0, 0)

    cache_k = jnp.transpose(cache_k, (0, 2, 1, 3))
    cache_v = jnp.transpose(cache_v, (0, 2, 1, 3))
    cache_specs = [pl.BlockSpec((1, n_heads, page, HEAD_DIM), page_map(g)) for g in range(pages)]
    grid_spec = pltpu.PrefetchScalarGridSpec(
        num_scalar_prefetch=1,
        grid=(b, n_pages // pages),
        in_specs=[pl.BlockSpec((1, main_w, 128), lambda i, j, pt: (i, 0, 0)),
                  pl.BlockSpec((1, 128), lambda i, j, pt: (0, 0)),
                  pl.BlockSpec((1, new_rows, main_w), lambda i, j, pt: (i, 0, 0)),
                  pl.BlockSpec((1, new_rows, main_w), lambda i, j, pt: (i, 0, 0))]
                 + cache_specs + cache_specs,
        out_specs=pl.BlockSpec((1, t, main_w), lambda i, j, pt: (i, 0, 0)),
        scratch_shapes=[pltpu.VMEM((1, 128), F32), pltpu.VMEM((n_cols, main_w), F32)],
    )
    return pl.pallas_call(
        functools.partial(_sb_dec_kernel, pages=pages, n_q=t, n_new=t),
        grid_spec=grid_spec,
        out_shape=jax.ShapeDtypeStruct((b, t, main_w), BF),
        compiler_params=_params(("parallel", "arbitrary")),
        name="stick_breaking_decode",
    )(page_table, qt, bias_c, k_pad, v_pad, *([cache_k] * pages), *([cache_v] * pages))


def _out_ln_kernel(ym_ref, ye_ref, x_ref, wa_ref, wb_ref, g_ref, b_ref, rc_ref, rh_ref,
                   x1_ref, xq_ref, lt_ref, hbuf, *, alpha):
    @pl.when(pl.program_id(0) == 0)
    def _():
        hbuf[...] = jnp.zeros_like(hbuf)

    tm, d = x_ref.shape
    half = d // LANES // 2
    x1 = _layer_norm(alpha * x_ref[...] + hbuf[...], g_ref[...], b_ref[...])
    x1_ref[...] = x1
    xh, xl = _split_bf16(x1)
    bits = lax.bitcast_convert_type(xh.astype(F32), jnp.uint32)
    for k in range(half):
        lo = bits[:, k * LANES:(k + 1) * LANES] >> 16
        hi = bits[:, (half + k) * LANES:(half + k + 1) * LANES] & jnp.uint32(0xFFFF0000)
        xq_ref[pl.ds(k, tm, stride=half), :] = lo | hi
    both = _dot(xh, rc_ref[...])
    lt = both[:, :LANES] + both[:, LANES:] + _dot(xl, rh_ref[...])
    lt_ref[...] = lt.T[:ROUTE_ROWS]
    hbuf[...] = _dot(ym_ref[...], wa_ref[...]) + _dot(ye_ref[...], wb_ref[...])


def _out_ln_route(y_main, y_mem, x, w_out_bf, layer, gamma, beta, r_cat, r_hi, alpha, tm):
    n, d = x.shape
    main_w = y_main.shape[1]
    mem_w = y_mem.shape[1]
    nt = n // tm
    half = d // LANES // 2
    cur = lambda i: (jnp.minimum(i, nt - 1), 0)
    prev = lambda i: (jnp.maximum(i - 1, 0), 0)
    return pl.pallas_call(
        functools.partial(_out_ln_kernel, alpha=alpha),
        grid=(nt + 1,),
        in_specs=[pl.BlockSpec((tm, main_w), cur),
                  pl.BlockSpec((tm, mem_w), cur),
                  pl.BlockSpec((tm, d), prev),
                  pl.BlockSpec((None, main_w, d), lambda i: (layer, 0, 0)),
                  pl.BlockSpec((None, mem_w, d), lambda i: (layer, main_w // mem_w, 0)),
                  pl.BlockSpec((1, d), lambda i: (0, 0)),
                  pl.BlockSpec((1, d), lambda i: (0, 0)),
                  pl.BlockSpec((d, 2 * LANES), lambda i: (0, 0)),
                  pl.BlockSpec((d, LANES), lambda i: (0, 0))],
        out_specs=[pl.BlockSpec((tm, d), prev),
                   pl.BlockSpec((tm * half, LANES), prev),
                   pl.BlockSpec((ROUTE_ROWS, tm), lambda i: (0, jnp.maximum(i - 1, 0)))],
        out_shape=[jax.ShapeDtypeStruct((n, d), F32),
                   jax.ShapeDtypeStruct((n * half, LANES), jnp.uint32),
                   jax.ShapeDtypeStruct((ROUTE_ROWS, n), F32)],
        scratch_shapes=[pltpu.VMEM((tm, d), F32)],
        compiler_params=_params(("arbitrary",)),
        name="out_ln_route",
    )(y_main, y_mem, x, w_out_bf, w_out_bf, gamma.reshape(1, d), beta.reshape(1, d), r_cat, r_hi)


def _route_kernel(lt_ref, rb_ref, e1_ref, e2_ref, g1_ref, g2_ref):
    lt = lt_ref[...] + rb_ref[...]
    ng, ne = N_GROUPS, EXP_PER_GROUP
    lg = [lt[g:g + 1, :] for g in range(ng)]
    m = lg[0]
    gi = jnp.zeros(m.shape, I32)
    for g in range(1, ng):
        better = lg[g] > m
        gi = jnp.where(better, g, gi)
        m = jnp.where(better, lg[g], m)
    den = jnp.exp(lg[0] - m)
    for g in range(1, ng):
        den = den + jnp.exp(lg[g] - m)
    p_sel = 1.0 / den
    le = []
    for e in range(ne):
        v = lt[ng + e:ng + e + 1, :]
        for g in range(1, ng):
            r = ng + g * ne + e
            v = jnp.where(gi == g, lt[r:r + 1, :], v)
        le.append(v)
    v1 = le[0]
    i1 = jnp.zeros(m.shape, I32)
    for e in range(1, ne):
        better = le[e] > v1
        i1 = jnp.where(better, e, i1)
        v1 = jnp.where(better, le[e], v1)
    v2 = jnp.full(m.shape, -jnp.inf, F32)
    i2 = jnp.zeros(m.shape, I32)
    for e in range(ne):
        better = (i1 != e) & (le[e] > v2)
        i2 = jnp.where(better, e, i2)
        v2 = jnp.where(better, le[e], v2)
    t = jnp.exp(v2 - v1)
    w_a = 1.0 / (1.0 + t)
    w_b = t / (1.0 + t)
    e1_ref[...] = gi * ne + i1
    e2_ref[...] = gi * ne + i2
    g1_ref[...] = p_sel * w_a
    g2_ref[...] = p_sel * w_b


def _route(lt, rbias):
    n = lt.shape[1]
    shp = lambda dt: jax.ShapeDtypeStruct((1, n), dt)
    return pl.pallas_call(
        _route_kernel,
        out_shape=[shp(I32), shp(I32), shp(F32), shp(F32)],
        compiler_params=pltpu.CompilerParams(vmem_limit_bytes=VMEM_LIMIT),
        name="route",
    )(lt, rbias)


def _moe_plan(e1, e2, tm, n_tiles):
    n = e1.shape[0]
    e = jnp.concatenate([e1, e2])
    oh = (e[:, None] == jnp.arange(N_EXPERTS, dtype=I32)[None, :]).astype(I32)
    csum = jnp.cumsum(oh, axis=0)
    rank = jnp.sum((csum - oh) * oh, axis=1)
    counts = csum[-1]
    tiles = (counts + tm - 1) // tm
    tile_end = jnp.cumsum(tiles)
    starts = (tile_end - tiles) * tm
    dest = jnp.sum(oh * starts[None, :], axis=1) + rank
    n_used = tile_end[-1]
    tidx = jnp.arange(n_tiles, dtype=I32)
    te = jnp.sum((tidx[:, None] >= tile_end[None, :]).astype(I32), axis=1)
    last_e = jnp.max(jnp.where(tiles > 0, jnp.arange(N_EXPERTS, dtype=I32), 0))
    te = jnp.where(tidx < n_used, te, last_e).astype(I32)
    rows = n_tiles * tm
    pair_of_row = jnp.full((rows,), -1, I32).at[dest].set(jnp.arange(2 * n, dtype=I32))
    has_pair = pair_of_row >= 0
    filler = jnp.logical_and(~has_pair, jnp.arange(rows, dtype=I32) < n_used * tm)
    spill = 2 * n + tm + jnp.cumsum(filler.astype(I32)) - 1
    src = jnp.where(has_pair, pair_of_row % n, 0).astype(I32)
    dst_rows = jnp.where(has_pair, pair_of_row, jnp.where(filler, spill, 2 * n))
    dst = jnp.concatenate([2 * n + jnp.arange(tm, dtype=I32), dst_rows]).astype(I32)
    return src, dst, te, n_used.reshape(1).astype(I32)


def _moe_kernel(te_ref, src_ref, dst_ref, nu_ref, x_hbm, wg_ref, wu_ref, wd_ref, y_hbm,
                xbuf, ybuf, gsem, ssem, wg_s, wu_s, wd_s, *, tm, n_pairs, n_spill_tiles):
    j = pl.program_id(0)
    n_used = nu_ref[0]
    slot = j % 3
    ahead = (j + 2) % 3
    chunks = wg_s.shape[0] // LANES
    half = chunks // 2
    tile_rows = tm * chunks

    def token_rows(token, rows):
        return pl.ds(pl.multiple_of(token * rows, rows), rows)

    def start_gather(tile, to_slot):
        for r in range(tm):
            pltpu.make_async_copy(x_hbm.at[token_rows(src_ref[tile * tm + r], half)],
                                  xbuf.at[token_rows(to_slot * tm + r, half)],
                                  gsem.at[to_slot]).start()

    def wait_gather(of_slot):
        pltpu.make_async_copy(x_hbm.at[pl.ds(0, tm * half)], xbuf.at[token_rows(of_slot, tm * half)],
                              gsem.at[of_slot]).wait()

    def start_scatter(tile, from_slot):
        for r in range(tm):
            pltpu.make_async_copy(ybuf.at[token_rows(from_slot * tm + r, chunks)],
                                  y_hbm.at[token_rows(dst_ref[(tile + 1) * tm + r], chunks)],
                                  ssem.at[from_slot]).start(priority=1)

    def wait_scatter(of_slot):
        pltpu.make_async_copy(ybuf.at[token_rows(of_slot, tile_rows)], y_hbm.at[pl.ds(0, tile_rows)],
                              ssem.at[of_slot]).wait()

    @pl.when(j == 0)
    def _():
        start_gather(0, 0)
        start_gather(1, 1)
        ybuf[pl.ds(2 * tile_rows, tile_rows), :] = jnp.zeros((tile_rows, LANES), F32)
        fills = [pltpu.make_async_copy(ybuf.at[pl.ds(2 * tile_rows, tile_rows)],
                                       y_hbm.at[pl.ds((n_pairs + c * tm) * chunks, tile_rows)],
                                       ssem.at[2]) for c in range(1, n_spill_tiles)]
        for f in fills:
            f.start()
        for f in fills:
            f.wait()

    e = te_ref[j]
    e_prev = te_ref[jnp.maximum(j - 1, 0)]

    @pl.when((j == 0) | (e != e_prev))
    def _():
        wg_s[...] = wg_ref[0, 0].astype(BF)
        wu_s[...] = wu_ref[0, 0].astype(BF)
        wd_s[...] = wd_ref[0, 0].astype(BF)

    @pl.when((j >= 2) & (j <= n_used + 2))
    def _():
        wait_scatter(slot)

    @pl.when(j < n_used)
    def _():
        wait_gather(slot)
        start_gather(j + 2, ahead)
        start_scatter(j - 1, ahead)
        base = slot * tile_rows
        words = [xbuf[pl.ds(slot * tm * half + k, tm, stride=half), :] for k in range(half)]
        unpack = lambda w: lax.bitcast_convert_type(w, F32).astype(BF)
        x = jnp.concatenate([unpack(w << 16) for w in words]
                            + [unpack(w & jnp.uint32(0xFFFF0000)) for w in words], axis=1)
        g = _dot(x, wg_s[...])
        u = _dot(x, wu_s[...])
        h = (g * jax.nn.sigmoid(g) * u).astype(BF)
        y = _dot(h, wd_s[...])
        for c in range(chunks):
            ybuf[pl.ds(base + c, tm, stride=chunks), :] = y[:, c * LANES:(c + 1) * LANES]

    @pl.when(j == n_used)
    def _():
        wait_gather(slot)
        start_scatter(j - 1, ahead)

    @pl.when(j == n_used + 1)
    def _():
        wait_gather(slot)


def _moe_experts(xq, src, dst, te, n_used, w_gate, w_up, w_down, layer, tm, n_tiles):
    d, ff = w_gate.shape[-2:]
    chunks = d // LANES
    n = xq.shape[0] * 2 // chunks
    n_spill_tiles = 1 + -(-N_EXPERTS * (tm - 1) // tm)
    grid_spec = pltpu.PrefetchScalarGridSpec(
        num_scalar_prefetch=4,
        grid=(n_tiles,),
        in_specs=[pl.BlockSpec(memory_space=pl.ANY),
                  pl.BlockSpec((1, 1, d, ff), lambda j, te, *_: (layer, te[j], 0, 0)),
                  pl.BlockSpec((1, 1, d, ff), lambda j, te, *_: (layer, te[j], 0, 0)),
                  pl.BlockSpec((1, 1, ff, d), lambda j, te, *_: (layer, te[j], 0, 0))],
        out_specs=pl.BlockSpec(memory_space=pl.ANY),
        scratch_shapes=[pltpu.VMEM((3 * tm * chunks // 2, LANES), jnp.uint32),
                        pltpu.VMEM((3 * tm * chunks, LANES), F32),
                        pltpu.SemaphoreType.DMA((3,)),
                        pltpu.SemaphoreType.DMA((3,)),
                        pltpu.VMEM((d, ff), BF),
                        pltpu.VMEM((d, ff), BF),
                        pltpu.VMEM((ff, d), BF)],
    )
    return pl.pallas_call(
        functools.partial(_moe_kernel, tm=tm, n_pairs=2 * n, n_spill_tiles=n_spill_tiles),
        grid_spec=grid_spec,
        out_shape=jax.ShapeDtypeStruct(((2 * n + n_spill_tiles * tm) * chunks, LANES), F32),
        compiler_params=_params(("arbitrary",)),
        name="moe_experts",
    )(te, src, dst, n_used, xq, w_gate, w_up, w_down)


def _combine_kernel(x1_ref, ya_ref, yb_ref, g1_ref, g2_ref, gam_ref, bet_ref,
                    x2_ref, x2b_ref, *, alpha):
    tm, d = x2_ref.shape
    chunks = d // LANES
    g1 = g1_ref[...]
    g2 = g2_ref[...]
    cols = []
    for c in range(chunks):
        rows = pl.ds(c, tm, stride=chunks)
        cols.append(alpha * x1_ref[:, c * LANES:(c + 1) * LANES]
                    + g1 * ya_ref[rows, :] + g2 * yb_ref[rows, :])
    x2 = _layer_norm(jnp.concatenate(cols, axis=1), gam_ref[...], bet_ref[...])
    x2_ref[...] = x2
    x2b_ref[...] = x2.astype(BF)


def _combine_ln(x1, y_pairs, g1, g2, gamma, beta, alpha, tm):
    n, d = x1.shape
    chunks = d // LANES
    nb = n // tm
    return pl.pallas_call(
        functools.partial(_combine_kernel, alpha=alpha),
        grid=(nb,),
        in_specs=[pl.BlockSpec((tm, d), lambda i: (i, 0)),
                  pl.BlockSpec((tm * chunks, LANES), lambda i: (i, 0)),
                  pl.BlockSpec((tm * chunks, LANES), lambda i: (i + nb, 0)),
                  pl.BlockSpec((tm, 1), lambda i: (i, 0)),
                  pl.BlockSpec((tm, 1), lambda i: (i, 0)),
                  pl.BlockSpec((1, d), lambda i: (0, 0)),
                  pl.BlockSpec((1, d), lambda i: (0, 0))],
        out_specs=[pl.BlockSpec((tm, d), lambda i: (i, 0)),
                   pl.BlockSpec((tm, d), lambda i: (i, 0))],
        out_shape=[jax.ShapeDtypeStruct((n, d), F32), jax.ShapeDtypeStruct((n, d), BF)],
        compiler_params=_params(("parallel",)),
        name="combine_ln",
    )(x1, y_pairs, y_pairs, g1.reshape(n, 1), g2.reshape(n, 1), gamma.reshape(1, d),
      beta.reshape(1, d))


def _hier_moe_ln(x1, xq, lt, rbias, w_gate, w_up, w_down, layer, gamma, beta, alpha, tm_moe,
                 tm_tok):
    n = lt.shape[1]
    e1, e2, g1, g2 = _route(lt, rbias)
    n_tiles = (2 * n + N_EXPERTS * (tm_moe - 1)) // tm_moe + 3
    src, dst, te, n_used = _moe_plan(e1[0], e2[0], tm_moe, n_tiles)
    y_pairs = _moe_experts(xq, src, dst, te, n_used, w_gate, w_up, w_down, layer, tm_moe, n_tiles)
    return _combine_ln(x1, y_pairs, g1[0], g2[0], gamma, beta, alpha, tm_tok)


def _trunk(x3, pool_prev, mem_k, mem_v, sb_past, p, *, decode):
    b, t, d = x3.shape
    depth = p["w_in"].shape[0]
    n_a = p["w_pool"].shape[0]
    main_w = p["w_sb_kv"].shape[-1] // 2
    mem_w = d - main_w
    n_heads = main_w // HEAD_DIM
    alpha = float((2 * depth) ** 0.25)
    n = b * t
    tm = min(512, n)
    tm_c = min(256, n)
    tm_moe = 256 if n >= 2048 else 16
    pos0 = 0 if not decode else sb_past[2].shape[1] * sb_past[0].shape[1]

    x = x3.reshape(n, d)
    xb = x
    new_pool = []
    k_bf = v_bf = k_f = v_f = None
    for l in range(depth):
        (proj,) = _matmul(xb, p["w_in"], l, [(0, d, F32)], tm, "proj_in")
        proj3 = proj.reshape(b, t, d)
        main = proj3[..., :main_w]
        if l < n_a:
            if decode:
                u_tm = jnp.transpose(main, (1, 0, 2))
                prev_tm = jnp.transpose(pool_prev[l], (1, 0, 2))
                y_tm = _pool_mix_decode(u_tm, prev_tm, p["w_pool"][l], p["pool_scale"][l], pos0)
                y_main = jnp.transpose(y_tm, (1, 0, 2))
            else:
                y_main = _pool_mix(proj3, pool_prev[l], p["w_pool"][l], p["pool_scale"][l],
                                   main_w, min(256, t), pos0)
            if t >= POOL_STATE:
                new_pool.append(main[:, t - POOL_STATE:, :])
            else:
                new_pool.append(jnp.concatenate([pool_prev[l][:, t:, :], main], axis=1))
        else:
            bias = p["sb_bias"][l - n_a]
            if decode:
                y_main = _stick_breaking_decode(main, k_f.reshape(b, t, main_w),
                                                v_f.reshape(b, t, main_w),
                                                sb_past[0], sb_past[1], sb_past[2], bias, pages=8)
            else:
                y_main = _stick_breaking_prompt(proj3, k_bf.reshape(b, t, main_w),
                                                v_bf.reshape(b, t, main_w), bias, n_heads, 256, 4)
        if decode:
            t_pad = 8
            proj_pad = jnp.pad(proj3, ((0, 0), (0, t_pad - t), (0, 0)))
            y_mem = _mem_attend(proj_pad, mem_k[l], mem_v[l], mem_w, t_pad)[:, :t]
        else:
            y_mem = _mem_attend(proj3, mem_k[l], mem_v[l], mem_w, min(512, t))
        x1, xq, lt = _out_ln_route(y_main.reshape(n, main_w), y_mem.reshape(n, mem_w), x,
                                   p["w_out"], l, p["ln1_g"][l], p["ln1_b"][l],
                                   p["r_cat"][l], p["r_hi"][l], alpha, tm_c)
        x, xb = _hier_moe_ln(x1, xq, lt, p["r_bias"][l], p["w_gate"], p["w_up"], p["w_down"], l,
                             p["ln2_g"][l], p["ln2_b"][l], alpha, tm_moe, tm)
        if l == n_a - 1:
            if decode:
                k_f, v_f = _matmul(xb, p["w_sb_kv"], 0, [(0, main_w, F32), (main_w, main_w, F32)],
                                   n, "proj_sb_kv")
                k_out = k_f.reshape(b, t, n_heads, HEAD_DIM)
                v_out = v_f.reshape(b, t, n_heads, HEAD_DIM)
            else:
                k_hm, v_hm, k_bf, v_bf = _kv_project(xb, p["w_sb_kv"][0], b, t, 256)
                k_out = jnp.transpose(k_hm, (0, 2, 1, 3))
                v_out = jnp.transpose(v_hm, (0, 2, 1, 3))
    return x.reshape(b, t, d), jnp.stack(new_pool, axis=0), k_out, v_out


def kernel(x_prompt, x_sample, state_pool, cache_sb_k, cache_sb_v, cache_mem_k, cache_mem_v,
           page_table, mem_prompt, w_in, w_out, w_pool_grp, pool_scale, w_mem_k, w_mem_v,
           ln1_g, ln1_b, ln2_g, ln2_b, w_route_grp, b_route_grp, w_route_exp, b_route_exp,
           w_gate, w_up, w_down, w_sb_k, w_sb_v, sb_bias):
    depth, d, _ = w_in.shape
    bp, _, _ = x_prompt.shape
    bs, ts, _ = x_sample.shape
    n_a = w_pool_grp.shape[0]
    main_w = w_sb_k.shape[1]
    mem_w = w_mem_k.shape[2]
    n_mem = mem_prompt.shape[1]
    n_heads = main_w // HEAD_DIM
    mem_heads = mem_w // HEAD_DIM

    wr = jnp.concatenate([w_route_grp, w_route_exp.reshape(depth, d, N_EXPERTS)], axis=-1)
    wr = jnp.pad(wr, ((0, 0), (0, 0), (0, LANES - N_GROUPS - N_EXPERTS)))
    r_hi = wr.astype(BF)
    r_lo = (wr - r_hi.astype(F32)).astype(BF)
    r_cat = jnp.concatenate([r_hi, r_lo], axis=-1)
    r_bias = jnp.pad(jnp.concatenate([b_route_grp, b_route_exp.reshape(depth, N_EXPERTS)], axis=-1),
                     ((0, 0), (0, ROUTE_ROWS - N_GROUPS - N_EXPERTS))).reshape(depth, ROUTE_ROWS, 1)
    p = dict(
        w_in=w_in.astype(BF), w_out=w_out.astype(BF), w_pool=w_pool_grp.astype(BF),
        pool_scale=pool_scale, w_sb_kv=jnp.concatenate([w_sb_k, w_sb_v], axis=1).astype(BF)[None],
        ln1_g=ln1_g, ln1_b=ln1_b, ln2_g=ln2_g, ln2_b=ln2_b, r_hi=r_hi, r_cat=r_cat, r_bias=r_bias,
        w_gate=w_gate, w_up=w_up, w_down=w_down, sb_bias=sb_bias)

    w_mem = jnp.concatenate([jnp.transpose(w_mem_k, (1, 0, 2)).reshape(d, depth * mem_w),
                             jnp.transpose(w_mem_v, (1, 0, 2)).reshape(d, depth * mem_w)],
                            axis=1).astype(BF)
    mem_kv = _mem_project(mem_prompt.reshape(bp * n_mem, d).astype(BF), w_mem, mem_w)
    mem_kv = mem_kv.reshape(2, depth, bp, n_mem, mem_w)
    pool0 = jnp.zeros((n_a, bp, POOL_STATE, main_w), x_prompt.dtype)
    y_p, pool_p, k_p, v_p = _trunk(x_prompt, pool0, mem_kv[0], mem_kv[1], None, p, decode=False)

    sb_past = (cache_sb_k, cache_sb_v, page_table)
    y_s, pool_s, k_s, v_s = _trunk(
        x_sample, state_pool, cache_mem_k.reshape(depth, bs, n_mem, mem_w),
        cache_mem_v.reshape(depth, bs, n_mem, mem_w), sb_past, p, decode=True)

    return (y_p, y_s, pool_p, pool_s, k_p, v_p, k_s, v_s,
            mem_kv[0].reshape(depth, bp, n_mem, mem_heads, HEAD_DIM),
            mem_kv[1].reshape(depth, bp, n_mem, mem_heads, HEAD_DIM))
```

```python
import functools

import jax
import jax.numpy as jnp
import numpy as np
from jax import lax
from jax.experimental import pallas as pl
from jax.experimental.pallas import tpu as pltpu

F32 = jnp.float32
BF = jnp.bfloat16
I32 = jnp.int32

HEAD_DIM = 128
LANES = 128
POOL_WINDOWS = (2, 4, 8, 16)
POOL_STATE = max(POOL_WINDOWS) - 1
N_GROUPS = 4
EXP_PER_GROUP = 4
N_EXPERTS = N_GROUPS * EXP_PER_GROUP
LN_EPS = 1e-5
ROUTE_ROWS = 32
VMEM_LIMIT = 56 * 2**20

_NT = (((1,), (1,)), ((), ()))


def _params(sem):
    return pltpu.CompilerParams(dimension_semantics=sem, vmem_limit_bytes=VMEM_LIMIT)


def _dot(a, b):
    return jnp.dot(a, b, preferred_element_type=F32)


def _layer_norm(v, g, b):
    mu = jnp.mean(v, axis=-1, keepdims=True)
    c = v - mu
    var = jnp.mean(c * c, axis=-1, keepdims=True)
    return c * lax.rsqrt(var + LN_EPS) * g + b


def _softplus(z):
    return jnp.maximum(z, 0.0) + jnp.log(1.0 + jnp.exp(jnp.minimum(z, -z)))


def _split_bf16(x):
    hi = x.astype(BF)
    lo = (x - hi.astype(F32)).astype(BF)
    return hi, lo


def _mm_kernel(x_ref, w_ref, *o_refs, offsets):
    acc = _dot(x_ref[...].astype(BF), w_ref[...])
    for o_ref, off in zip(o_refs, offsets):
        o_ref[...] = acc[:, off:off + o_ref.shape[-1]].astype(o_ref.dtype)


def _matmul(x, w, layer, outs, tm, name):
    m, k = x.shape
    n = w.shape[-1]
    return pl.pallas_call(
        functools.partial(_mm_kernel, offsets=tuple(o for o, _, _ in outs)),
        grid=(m // tm,),
        in_specs=[pl.BlockSpec((tm, k), lambda i: (i, 0)),
                  pl.BlockSpec((None, k, n), lambda i: (layer, 0, 0))],
        out_specs=[pl.BlockSpec((tm, c), lambda i: (i, 0)) for _, c, _ in outs],
        out_shape=[jax.ShapeDtypeStruct((m, c), dt) for _, c, dt in outs],
        compiler_params=_params(("parallel",)),
        name=name,
    )(x, w)


def _kv_kernel(x_ref, w_ref, kf_ref, vf_ref, kb_ref, vb_ref):
    n_heads = kf_ref.shape[1]
    main_w = n_heads * HEAD_DIM
    acc = _dot(x_ref[...], w_ref[...])
    for h in range(n_heads):
        kf_ref[0, h] = acc[:, h * HEAD_DIM:(h + 1) * HEAD_DIM]
        vf_ref[0, h] = acc[:, main_w + h * HEAD_DIM:main_w + (h + 1) * HEAD_DIM]
    kb_ref[...] = acc[:, :main_w].astype(BF)
    vb_ref[...] = acc[:, main_w:].astype(BF)


def _kv_project(xb, w_kv, b, t, tm):
    n, d = xb.shape
    main_w = w_kv.shape[1] // 2
    n_heads = main_w // HEAD_DIM
    nt = t // tm
    head_major = pl.BlockSpec((1, n_heads, tm, HEAD_DIM), lambda i: (i // nt, 0, i % nt, 0))
    return pl.pallas_call(
        _kv_kernel,
        grid=(n // tm,),
        in_specs=[pl.BlockSpec((tm, d), lambda i: (i, 0)),
                  pl.BlockSpec((d, 2 * main_w), lambda i: (0, 0))],
        out_specs=[head_major, head_major,
                   pl.BlockSpec((tm, main_w), lambda i: (i, 0)),
                   pl.BlockSpec((tm, main_w), lambda i: (i, 0))],
        out_shape=[jax.ShapeDtypeStruct((b, n_heads, t, HEAD_DIM), F32)] * 2
                  + [jax.ShapeDtypeStruct((n, main_w), BF)] * 2,
        compiler_params=_params(("parallel",)),
        name="proj_sb_kv",
    )(xb, w_kv)


def _memproj_kernel(x_ref, w_ref, o_ref):
    o_ref[0] = _dot(x_ref[...], w_ref[...])


def _mem_project(mem_bf, w_cat, width):
    m, d = mem_bf.shape
    nj = w_cat.shape[1] // width
    return pl.pallas_call(
        _memproj_kernel,
        grid=(nj,),
        in_specs=[pl.BlockSpec((m, d), lambda j: (0, 0)),
                  pl.BlockSpec((d, width), lambda j: (0, j))],
        out_specs=pl.BlockSpec((1, m, width), lambda j: (j, 0, 0)),
        out_shape=jax.ShapeDtypeStruct((nj, m, width), F32),
        compiler_params=_params(("parallel",)),
        name="mem_project",
    )(mem_bf, w_cat)


def _pool_kernel(u_ref, prev_ref, w_ref, sc_ref, y_ref, buf, *, tt, nt, pos0):
    t = pl.program_id(1)
    p = POOL_STATE
    cg = w_ref.shape[-1]

    @pl.when(t == 0)
    def _():
        buf[1:1 + p, :] = prev_ref[0]

    u = u_ref[0]
    buf[p + 1:p + 1 + tt, :] = u
    pos = lax.broadcasted_iota(I32, (tt, 1), 0) + (t * tt + pos0)
    for g, w in enumerate(POOL_WINDOWS):
        c0 = g * cg
        ug = u[:, c0:c0 + cg]
        s = ug
        for k in range(1, w):
            s = s + buf[p + 1 - k:p + 1 - k + tt, c0:c0 + cg]
        cnt = jnp.minimum(w, pos + 1).astype(F32)
        d = (s / cnt - ug).astype(BF)
        yg = _dot(d, w_ref[g]) * sc_ref[:, c0:c0 + cg]
        y_ref[0, :, c0:c0 + cg] = yg.astype(y_ref.dtype)
    if nt > 1:
        buf[1:1 + p, :] = buf[tt + 1:tt + 1 + p, :]


def _pool_mix(proj, prev, w_grp_bf, scale, main_w, tt, pos0):
    b, t, _ = proj.shape
    nt = t // tt
    cg = main_w // len(POOL_WINDOWS)
    return pl.pallas_call(
        functools.partial(_pool_kernel, tt=tt, nt=nt, pos0=pos0),
        grid=(b, nt),
        in_specs=[pl.BlockSpec((1, tt, main_w), lambda i, j: (i, j, 0)),
                  pl.BlockSpec((1, POOL_STATE, main_w), lambda i, j: (i, 0, 0)),
                  pl.BlockSpec((len(POOL_WINDOWS), cg, cg), lambda i, j: (0, 0, 0)),
                  pl.BlockSpec((1, main_w), lambda i, j: (0, 0))],
        out_specs=pl.BlockSpec((1, tt, main_w), lambda i, j: (i, j, 0)),
        out_shape=jax.ShapeDtypeStruct((b, t, main_w), BF),
        scratch_shapes=[pltpu.VMEM((POOL_STATE + 1 + tt, main_w), F32)],
        compiler_params=_params(("parallel", "arbitrary")),
        name="pool_mix",
    )(proj, prev, w_grp_bf, scale.reshape(1, main_w))


def _pool_dec_kernel(u_ref, prev_ref, w_ref, sc_ref, y_ref, *, pos0):
    p = POOL_STATE
    nt = u_ref.shape[0]
    cg = w_ref.shape[-1]
    cat = [prev_ref[k] for k in range(p)] + [u_ref[i] for i in range(nt)]
    for g, w in enumerate(POOL_WINDOWS):
        c0 = g * cg
        ds = []
        for i in range(nt):
            s = cat[p + i][:, c0:c0 + cg]
            for k in range(1, w):
                s = s + cat[p + i - k][:, c0:c0 + cg]
            cnt = float(min(w, pos0 + i + 1))
            ds.append(s / cnt - cat[p + i][:, c0:c0 + cg])
        d = jnp.concatenate(ds, axis=0).astype(BF)
        yg = _dot(d, w_ref[g]) * sc_ref[:, c0:c0 + cg]
        nb = ds[0].shape[0]
        for i in range(nt):
            y_ref[i, :, c0:c0 + cg] = yg[i * nb:(i + 1) * nb].astype(y_ref.dtype)


def _pool_mix_decode(u_tm, prev_tm, w_grp_bf, scale, pos0):
    nt, b, c = u_tm.shape
    return pl.pallas_call(
        functools.partial(_pool_dec_kernel, pos0=pos0),
        out_shape=jax.ShapeDtypeStruct((nt, b, c), BF),
        compiler_params=pltpu.CompilerParams(vmem_limit_bytes=VMEM_LIMIT),
        name="pool_mix_decode",
    )(u_tm, prev_tm, w_grp_bf, scale.reshape(1, c))


def _mem_kernel(q_ref, k_ref, v_ref, o_ref, *, cdt):
    nh = q_ref.shape[-1] // HEAD_DIM
    scale = HEAD_DIM ** -0.5
    for h in range(nh):
        sl = slice(h * HEAD_DIM, (h + 1) * HEAD_DIM)
        q = q_ref[0, :, sl].astype(cdt)
        k = k_ref[0, :, sl].astype(cdt)
        v = v_ref[0, :, sl].astype(cdt)
        s = lax.dot_general(q, k, _NT, preferred_element_type=F32) * scale
        e = jnp.exp(s - jnp.max(s, axis=-1, keepdims=True))
        den = jnp.sum(e, axis=-1, keepdims=True)
        o = _dot(e.astype(cdt), v) / den
        o_ref[0, :, sl] = o.astype(o_ref.dtype)


def _mem_attend(proj, mk, mv, mem_w, tt):
    b, t, mix_w = proj.shape
    n_mem = mk.shape[1]
    qblk = (mix_w - mem_w) // mem_w
    cdt = BF if tt % 16 == 0 else F32
    return pl.pallas_call(
        functools.partial(_mem_kernel, cdt=cdt),
        grid=(b, t // tt),
        in_specs=[pl.BlockSpec((1, tt, mem_w), lambda i, j: (i, j, qblk)),
                  pl.BlockSpec((1, n_mem, mem_w), lambda i, j: (i, 0, 0)),
                  pl.BlockSpec((1, n_mem, mem_w), lambda i, j: (i, 0, 0))],
        out_specs=pl.BlockSpec((1, tt, mem_w), lambda i, j: (i, j, 0)),
        out_shape=jax.ShapeDtypeStruct((b, t, mem_w), BF),
        compiler_params=_params(("parallel", "parallel")),
        name="mem_attend",
    )(proj, mk, mv)


def _sb_kernel(q_ref, k_ref, v_ref, bias_ref, o_ref, *, tq, hp):
    qi = pl.program_id(2)
    inv_sqrt = 1.0 / np.sqrt(np.float32(HEAD_DIM))
    jj = lax.broadcasted_iota(I32, (tq, tq), 0)
    ss = lax.broadcasted_iota(I32, (tq, tq), 1)
    later = (jj > ss).astype(BF)
    later2 = jnp.concatenate([later, later], axis=0)
    causal = ss < jj
    heads = [slice(h * HEAD_DIM, (h + 1) * HEAD_DIM) for h in range(hp)]
    qs = [(q_ref[0, :, sl] * inv_sqrt).astype(BF) for sl in heads]
    biases = [bias_ref[0, h:h + 1, :] for h in range(hp)]

    def blocks(start, state, mask):
        rng = range(hp)
        zs = [lax.dot_general(qs[h], k_ref[0, pl.ds(start, tq), heads[h]], _NT,
                              preferred_element_type=F32) + biases[h] for h in rng]
        sps = [_softplus(z) for z in zs]
        if mask is not None:
            sps = [jnp.where(mask, sp, 0.0) for sp in sps]
        splits = [_split_bf16(sp) for sp in sps]
        later_sums = [_dot(jnp.concatenate([hi, lo], axis=1), later2) for hi, lo in splits]
        ws = [jnp.exp(zs[h] - sps[h] - later_sums[h] + state[h][0]) for h in rng]
        if mask is not None:
            ws = [jnp.where(mask, a, 0.0) for a in ws]
        accs = [state[h][1] + _dot(ws[h].astype(BF), v_ref[0, pl.ds(start, tq), heads[h]])
                for h in rng]
        tails = [state[h][0] - jnp.sum(sps[h], axis=1, keepdims=True) for h in rng]
        return tuple(zip(tails, accs))

    diag = pl.multiple_of(qi * tq, tq)
    zero = (jnp.zeros((tq, 1), F32), jnp.zeros((tq, HEAD_DIM), F32))
    state = blocks(diag, (zero,) * hp, causal)

    def body(n, carry):
        return blocks(pl.multiple_of((qi - 1 - n) * tq, tq), carry, None)

    state = lax.fori_loop(0, qi, body, state)
    for h in range(hp):
        o_ref[0, :, heads[h]] = state[h][1].astype(o_ref.dtype)


def _stick_breaking_prompt(proj, k_bf, v_bf, bias, n_heads, tq, hp):
    b, t, _ = proj.shape
    main_w = n_heads * HEAD_DIM
    w = hp * HEAD_DIM
    bias_b = jnp.broadcast_to(bias.astype(F32).reshape(n_heads // hp, hp, 1),
                              (n_heads // hp, hp, tq))
    return pl.pallas_call(
        functools.partial(_sb_kernel, tq=tq, hp=hp),
        grid=(b, n_heads // hp, t // tq),
        in_specs=[pl.BlockSpec((1, tq, w), lambda i, h, j: (i, j, h)),
                  pl.BlockSpec((1, t, w), lambda i, h, j: (i, 0, h)),
                  pl.BlockSpec((1, t, w), lambda i, h, j: (i, 0, h)),
                  pl.BlockSpec((1, hp, tq), lambda i, h, j: (h, 0, 0))],
        out_specs=pl.BlockSpec((1, tq, w), lambda i, h, j: (i, j, h)),
        out_shape=jax.ShapeDtypeStruct((b, t, main_w), BF),
        compiler_params=_params(("parallel", "parallel", "arbitrary")),
        name="stick_breaking_prompt",
    )(proj, k_bf, v_bf, bias_b)


def _sb_dec_kernel(pt_ref, qt_ref, bias_ref, kn_ref, vn_ref, *rest, pages, n_q, n_new):
    k_refs = rest[:pages]
    v_refs = rest[pages:2 * pages]
    o_ref, tail_ref, acc_ref = rest[2 * pages:]
    j = pl.program_id(1)
    inv_sqrt = 1.0 / np.sqrt(np.float32(HEAD_DIM))
    qt = qt_ref[0]
    bias = bias_ref[...]
    n_cols = acc_ref.shape[0]
    nk = kn_ref.shape[1]
    n_heads = k_refs[0].shape[1]
    ss = lax.broadcasted_iota(I32, (nk, nk), 0)
    jj = lax.broadcasted_iota(I32, (nk, nk), 1)
    later = (jj > ss).astype(BF)

    def heads_cat(ref):
        return jnp.concatenate([ref[0, h] for h in range(n_heads)], axis=1).astype(BF)

    def scores(kcat, mask):
        z = _dot(kcat, qt) * inv_sqrt + bias
        sp = _softplus(z)
        if mask is not None:
            sp = jnp.where(mask, sp, 0.0)
        hi, lo = _split_bf16(sp)
        both = _dot(later, jnp.concatenate([hi, lo], axis=1))
        later_sum = both[:, :128] + both[:, 128:]
        return z, sp, later_sum, jnp.sum(sp, axis=0, keepdims=True)

    def weights(z, sp, later_sum, tail, mask):
        a = jnp.exp(z - sp - later_sum + tail)
        if mask is not None:
            a = jnp.where(mask, a, 0.0)
        return a.T[:n_cols].astype(BF)

    @pl.when(j == 0)
    def _():
        r = lax.broadcasted_iota(I32, (nk, 128), 0)
        c = lax.broadcasted_iota(I32, (nk, 128), 1)
        mask = (r < c % n_q) & (r < n_new)
        z, sp, later_sum, total = scores(kn_ref[0].astype(BF), mask)
        a_t = weights(z, sp, later_sum, jnp.zeros_like(total), mask)
        acc_ref[...] = _dot(a_t, vn_ref[0].astype(BF))
        tail_ref[...] = -total

    parts = [scores(heads_cat(k_refs[g]), None) for g in range(pages)]
    tail = tail_ref[...]
    a_ts = []
    for z, sp, later_sum, total in parts:
        a_ts.append(weights(z, sp, later_sum, tail, None))
        tail = tail - total
    v_all = jnp.concatenate([heads_cat(v_refs[g]) for g in range(pages)], axis=0)
    acc_ref[...] += _dot(jnp.concatenate(a_ts, axis=1), v_all)
    tail_ref[...] = tail

    @pl.when(j == pl.num_programs(1) - 1)
    def _():
        for h in range(o_ref.shape[-1] // HEAD_DIM):
            sl = slice(h * HEAD_DIM, (h + 1) * HEAD_DIM)
            o_ref[0, :, sl] = acc_ref[h * n_q:(h + 1) * n_q, sl].astype(o_ref.dtype)


def _stick_breaking_decode(q, k_new, v_new, cache_k, cache_v, page_table, bias, pages):
    b, t, main_w = q.shape
    n_pages = page_table.shape[1]
    _, page, n_heads, _ = cache_k.shape
    n_cols = n_heads * t
    qh = q.reshape(b, t, n_heads, HEAD_DIM).astype(F32)
    eye = jnp.eye(n_heads, dtype=F32)
    qt = jnp.einsum("bihd,hg->bhdgi", qh, eye).reshape(b, main_w, n_cols)
    qt = jnp.pad(qt, ((0, 0), (0, 0), (0, 128 - n_cols))).astype(BF)
    bias_c = jnp.pad(jnp.repeat(bias.astype(F32), t), (0, 128 - n_cols)).reshape(1, 128)
    new_rows = page
    k_pad = jnp.pad(k_new, ((0, 0), (0, new_rows - t), (0, 0)))
    v_pad = jnp.pad(v_new, ((0, 0), (0, new_rows - t), (0, 0)))

    def page_map(g):
        return lambda i, j, pt: (pt[i, n_pages - 1 - (j * pages + g)], 0, 0, 0)

    cache_k = jnp.transpose(cache_k, (0, 2, 1, 3))
    cache_v = jnp.transpose(cache_v, (0, 2, 1, 3))
    cache_specs = [pl.BlockSpec((1, n_heads, page, HEAD_DIM), page_map(g)) for g in range(pages)]
    grid_spec = pltpu.PrefetchScalarGridSpec(
        num_scalar_prefetch=1,
        grid=(b, n_pages // pages),
        in_specs=[pl.BlockSpec((1, main_w, 128), lambda i, j, pt: (i, 0, 0)),
                  pl.BlockSpec((1, 128), lambda i, j, pt: (0, 0)),
                  pl.BlockSpec((1, new_rows, main_w), lambda i, j, pt: (i, 0, 0)),
                  pl.BlockSpec((1, new_rows, main_w), lambda i, j, pt: (i, 0, 0))]
                 + cache_specs + cache_specs,
        out_specs=pl.BlockSpec((1, t, main_w), lambda i, j, pt: (i, 0, 0)),
        scratch_shapes=[pltpu.VMEM((1, 128), F32), pltpu.VMEM((n_cols, main_w), F32)],
    )
    return pl.pallas_call(
        functools.partial(_sb_dec_kernel, pages=pages, n_q=t, n_new=t),
        grid_spec=grid_spec,
        out_shape=jax.ShapeDtypeStruct((b, t, main_w), BF),
        compiler_params=_params(("parallel", "arbitrary")),
        name="stick_breaking_decode",
    )(page_table, qt, bias_c, k_pad, v_pad, *([cache_k] * pages), *([cache_v] * pages))


def _out_ln_kernel(ym_ref, ye_ref, x_ref, wa_ref, wb_ref, g_ref, b_ref, rc_ref, rh_ref,
                   x1_ref, lt_ref, hbuf, *, alpha):
    @pl.when(pl.program_id(0) == 0)
    def _():
        hbuf[...] = jnp.zeros_like(hbuf)

    tm, d = x_ref.shape
    chunks = d // LANES
    x1 = _layer_norm(alpha * x_ref[...] + hbuf[...], g_ref[...], b_ref[...])
    for c in range(chunks):
        x1_ref[pl.ds(c, tm, stride=chunks), :] = x1[:, c * LANES:(c + 1) * LANES]
    xh, xl = _split_bf16(x1)
    both = _dot(xh, rc_ref[...])
    lt = both[:, :LANES] + both[:, LANES:] + _dot(xl, rh_ref[...])
    lt_ref[...] = lt.T[:ROUTE_ROWS]
    hbuf[...] = _dot(ym_ref[...], wa_ref[...]) + _dot(ye_ref[...], wb_ref[...])


def _out_ln_route(y_main, y_mem, x, w_out_bf, layer, gamma, beta, r_cat, r_hi, alpha, tm):
    n, d = x.shape
    main_w = y_main.shape[1]
    mem_w = y_mem.shape[1]
    nt = n // tm
    chunks = d // LANES
    cur = lambda i: (jnp.minimum(i, nt - 1), 0)
    prev = lambda i: (jnp.maximum(i - 1, 0), 0)
    return pl.pallas_call(
        functools.partial(_out_ln_kernel, alpha=alpha),
        grid=(nt + 1,),
        in_specs=[pl.BlockSpec((tm, main_w), cur),
                  pl.BlockSpec((tm, mem_w), cur),
                  pl.BlockSpec((tm, d), prev),
                  pl.BlockSpec((None, main_w, d), lambda i: (layer, 0, 0)),
                  pl.BlockSpec((None, mem_w, d), lambda i: (layer, main_w // mem_w, 0)),
                  pl.BlockSpec((1, d), lambda i: (0, 0)),
                  pl.BlockSpec((1, d), lambda i: (0, 0)),
                  pl.BlockSpec((d, 2 * LANES), lambda i: (0, 0)),
                  pl.BlockSpec((d, LANES), lambda i: (0, 0))],
        out_specs=[pl.BlockSpec((tm * chunks, LANES), prev),
                   pl.BlockSpec((ROUTE_ROWS, tm), lambda i: (0, jnp.maximum(i - 1, 0)))],
        out_shape=[jax.ShapeDtypeStruct((n * chunks, LANES), F32),
                   jax.ShapeDtypeStruct((ROUTE_ROWS, n), F32)],
        scratch_shapes=[pltpu.VMEM((tm, d), F32)],
        compiler_params=_params(("arbitrary",)),
        name="out_ln_route",
    )(y_main, y_mem, x, w_out_bf, w_out_bf, gamma.reshape(1, d), beta.reshape(1, d), r_cat, r_hi)


def _route_kernel(lt_ref, rb_ref, e1_ref, e2_ref, g1_ref, g2_ref):
    lt = lt_ref[...] + rb_ref[...]
    ng, ne = N_GROUPS, EXP_PER_GROUP
    lg = [lt[g:g + 1, :] for g in range(ng)]
    m = lg[0]
    gi = jnp.zeros(m.shape, I32)
    for g in range(1, ng):
        better = lg[g] > m
        gi = jnp.where(better, g, gi)
        m = jnp.where(better, lg[g], m)
    den = jnp.exp(lg[0] - m)
    for g in range(1, ng):
        den = den + jnp.exp(lg[g] - m)
    p_sel = 1.0 / den
    le = []
    for e in range(ne):
        v = lt[ng + e:ng + e + 1, :]
        for g in range(1, ng):
            r = ng + g * ne + e
            v = jnp.where(gi == g, lt[r:r + 1, :], v)
        le.append(v)
    v1 = le[0]
    i1 = jnp.zeros(m.shape, I32)
    for e in range(1, ne):
        better = le[e] > v1
        i1 = jnp.where(better, e, i1)
        v1 = jnp.where(better, le[e], v1)
    v2 = jnp.full(m.shape, -jnp.inf, F32)
    i2 = jnp.zeros(m.shape, I32)
    for e in range(ne):
        better = (i1 != e) & (le[e] > v2)
        i2 = jnp.where(better, e, i2)
        v2 = jnp.where(better, le[e], v2)
    t = jnp.exp(v2 - v1)
    w_a = 1.0 / (1.0 + t)
    w_b = t / (1.0 + t)
    e1_ref[...] = gi * ne + i1
    e2_ref[...] = gi * ne + i2
    g1_ref[...] = p_sel * w_a
    g2_ref[...] = p_sel * w_b


def _route(lt, rbias):
    n = lt.shape[1]
    shp = lambda dt: jax.ShapeDtypeStruct((1, n), dt)
    return pl.pallas_call(
        _route_kernel,
        out_shape=[shp(I32), shp(I32), shp(F32), shp(F32)],
        compiler_params=pltpu.CompilerParams(vmem_limit_bytes=VMEM_LIMIT),
        name="route",
    )(lt, rbias)


def _moe_plan(e1, e2, tm, n_tiles):
    n = e1.shape[0]
    e = jnp.concatenate([e1, e2])
    oh = (e[:, None] == jnp.arange(N_EXPERTS, dtype=I32)[None, :]).astype(I32)
    csum = jnp.cumsum(oh, axis=0)
    rank = jnp.sum((csum - oh) * oh, axis=1)
    counts = csum[-1]
    tiles = (counts + tm - 1) // tm
    tile_end = jnp.cumsum(tiles)
    starts = (tile_end - tiles) * tm
    dest = jnp.sum(oh * starts[None, :], axis=1) + rank
    n_used = tile_end[-1]
    tidx = jnp.arange(n_tiles, dtype=I32)
    te = jnp.sum((tidx[:, None] >= tile_end[None, :]).astype(I32), axis=1)
    last_e = jnp.max(jnp.where(tiles > 0, jnp.arange(N_EXPERTS, dtype=I32), 0))
    te = jnp.where(tidx < n_used, te, last_e).astype(I32)
    rows = n_tiles * tm
    pair_of_row = jnp.full((rows,), -1, I32).at[dest].set(jnp.arange(2 * n, dtype=I32))
    has_pair = pair_of_row >= 0
    filler = jnp.logical_and(~has_pair, jnp.arange(rows, dtype=I32) < n_used * tm)
    spill = 2 * n + tm + jnp.cumsum(filler.astype(I32)) - 1
    src = jnp.where(has_pair, pair_of_row % n, 0).astype(I32)
    dst_rows = jnp.where(has_pair, pair_of_row, jnp.where(filler, spill, 2 * n))
    dst = jnp.concatenate([2 * n + jnp.arange(tm, dtype=I32), dst_rows]).astype(I32)
    return src, dst, te, n_used.reshape(1).astype(I32)


def _moe_kernel(te_ref, src_ref, dst_ref, nu_ref, x_hbm, wg_ref, wu_ref, wd_ref, y_hbm,
                xbuf, ybuf, gsem, ssem, wg_s, wu_s, wd_s, *, tm, n_pairs, n_spill_tiles):
    j = pl.program_id(0)
    n_used = nu_ref[0]
    slot = j % 3
    ahead = (j + 2) % 3
    chunks = wg_s.shape[0] // LANES
    tile_rows = tm * chunks

    def token_rows(token, rows):
        return pl.ds(pl.multiple_of(token * rows, rows), rows)

    def start_gather(tile, to_slot):
        for r in range(tm):
            pltpu.make_async_copy(x_hbm.at[token_rows(src_ref[tile * tm + r], chunks)],
                                  xbuf.at[token_rows(to_slot * tm + r, chunks)],
                                  gsem.at[to_slot]).start()

    def wait_gather(of_slot):
        pltpu.make_async_copy(x_hbm.at[pl.ds(0, tile_rows)], xbuf.at[token_rows(of_slot, tile_rows)],
                              gsem.at[of_slot]).wait()

    def start_scatter(tile, from_slot):
        for r in range(tm):
            pltpu.make_async_copy(ybuf.at[token_rows(from_slot * tm + r, chunks)],
                                  y_hbm.at[token_rows(dst_ref[(tile + 1) * tm + r], chunks)],
                                  ssem.at[from_slot]).start()

    def wait_scatter(of_slot):
        pltpu.make_async_copy(ybuf.at[token_rows(of_slot, tile_rows)], y_hbm.at[pl.ds(0, tile_rows)],
                              ssem.at[of_slot]).wait()

    @pl.when(j == 0)
    def _():
        start_gather(0, 0)
        start_gather(1, 1)
        ybuf[pl.ds(2 * tile_rows, tile_rows), :] = jnp.zeros((tile_rows, LANES), F32)
        fills = [pltpu.make_async_copy(ybuf.at[pl.ds(2 * tile_rows, tile_rows)],
                                       y_hbm.at[pl.ds((n_pairs + c * tm) * chunks, tile_rows)],
                                       ssem.at[2]) for c in range(1, n_spill_tiles)]
        for f in fills:
            f.start()
        for f in fills:
            f.wait()

    e = te_ref[j]
    e_prev = te_ref[jnp.maximum(j - 1, 0)]

    @pl.when((j == 0) | (e != e_prev))
    def _():
        wg_s[...] = wg_ref[0, 0].astype(BF)
        wu_s[...] = wu_ref[0, 0].astype(BF)
        wd_s[...] = wd_ref[0, 0].astype(BF)

    @pl.when((j >= 2) & (j <= n_used + 2))
    def _():
        wait_scatter(slot)

    @pl.when(j < n_used)
    def _():
        wait_gather(slot)
        start_gather(j + 2, ahead)
        start_scatter(j - 1, ahead)
        base = slot * tile_rows
        x = jnp.concatenate([xbuf[pl.ds(base + c, tm, stride=chunks), :] for c in range(chunks)],
                            axis=1).astype(BF)
        g = _dot(x, wg_s[...])
        u = _dot(x, wu_s[...])
        h = (g * jax.nn.sigmoid(g) * u).astype(BF)
        y = _dot(h, wd_s[...])
        for c in range(chunks):
            ybuf[pl.ds(base + c, tm, stride=chunks), :] = y[:, c * LANES:(c + 1) * LANES]

    @pl.when(j == n_used)
    def _():
        wait_gather(slot)
        start_scatter(j - 1, ahead)

    @pl.when(j == n_used + 1)
    def _():
        wait_gather(slot)


def _moe_experts(x1, src, dst, te, n_used, w_gate, w_up, w_down, layer, tm, n_tiles):
    d, ff = w_gate.shape[-2:]
    chunks = d // LANES
    n = x1.shape[0] // chunks
    n_spill_tiles = 1 + -(-N_EXPERTS * (tm - 1) // tm)
    grid_spec = pltpu.PrefetchScalarGridSpec(
        num_scalar_prefetch=4,
        grid=(n_tiles,),
        in_specs=[pl.BlockSpec(memory_space=pl.ANY),
                  pl.BlockSpec((1, 1, d, ff), lambda j, te, *_: (layer, te[j], 0, 0)),
                  pl.BlockSpec((1, 1, d, ff), lambda j, te, *_: (layer, te[j], 0, 0)),
                  pl.BlockSpec((1, 1, ff, d), lambda j, te, *_: (layer, te[j], 0, 0))],
        out_specs=pl.BlockSpec(memory_space=pl.ANY),
        scratch_shapes=[pltpu.VMEM((3 * tm * chunks, LANES), F32),
                        pltpu.VMEM((3 * tm * chunks, LANES), F32),
                        pltpu.SemaphoreType.DMA((3,)),
                        pltpu.SemaphoreType.DMA((3,)),
                        pltpu.VMEM((d, ff), BF),
                        pltpu.VMEM((d, ff), BF),
                        pltpu.VMEM((ff, d), BF)],
    )
    return pl.pallas_call(
        functools.partial(_moe_kernel, tm=tm, n_pairs=2 * n, n_spill_tiles=n_spill_tiles),
        grid_spec=grid_spec,
        out_shape=jax.ShapeDtypeStruct(((2 * n + n_spill_tiles * tm) * chunks, LANES), F32),
        compiler_params=_params(("arbitrary",)),
        name="moe_experts",
    )(te, src, dst, n_used, x1, w_gate, w_up, w_down)


def _combine_kernel(x1_ref, ya_ref, yb_ref, g1_ref, g2_ref, gam_ref, bet_ref,
                    x2_ref, x2b_ref, *, alpha):
    tm, d = x2_ref.shape
    chunks = d // LANES
    g1 = g1_ref[...]
    g2 = g2_ref[...]
    cols = []
    for c in range(chunks):
        rows = pl.ds(c, tm, stride=chunks)
        cols.append(alpha * x1_ref[rows, :] + g1 * ya_ref[rows, :] + g2 * yb_ref[rows, :])
    x2 = _layer_norm(jnp.concatenate(cols, axis=1), gam_ref[...], bet_ref[...])
    x2_ref[...] = x2
    x2b_ref[...] = x2.astype(BF)


def _combine_ln(x1, y_pairs, g1, g2, gamma, beta, alpha, tm):
    d = gamma.shape[0]
    chunks = d // LANES
    n = x1.shape[0] // chunks
    nb = n // tm
    return pl.pallas_call(
        functools.partial(_combine_kernel, alpha=alpha),
        grid=(nb,),
        in_specs=[pl.BlockSpec((tm * chunks, LANES), lambda i: (i, 0)),
                  pl.BlockSpec((tm * chunks, LANES), lambda i: (i, 0)),
                  pl.BlockSpec((tm * chunks, LANES), lambda i: (i + nb, 0)),
                  pl.BlockSpec((tm, 1), lambda i: (i, 0)),
                  pl.BlockSpec((tm, 1), lambda i: (i, 0)),
                  pl.BlockSpec((1, d), lambda i: (0, 0)),
                  pl.BlockSpec((1, d), lambda i: (0, 0))],
        out_specs=[pl.BlockSpec((tm, d), lambda i: (i, 0)),
                   pl.BlockSpec((tm, d), lambda i: (i, 0))],
        out_shape=[jax.ShapeDtypeStruct((n, d), F32), jax.ShapeDtypeStruct((n, d), BF)],
        compiler_params=_params(("parallel",)),
        name="combine_ln",
    )(x1, y_pairs, y_pairs, g1.reshape(n, 1), g2.reshape(n, 1), gamma.reshape(1, d),
      beta.reshape(1, d))


def _hier_moe_ln(x1, lt, rbias, w_gate, w_up, w_down, layer, gamma, beta, alpha, tm_moe, tm_tok):
    n = lt.shape[1]
    e1, e2, g1, g2 = _route(lt, rbias)
    n_tiles = (2 * n + N_EXPERTS * (tm_moe - 1)) // tm_moe + 3
    src, dst, te, n_used = _moe_plan(e1[0], e2[0], tm_moe, n_tiles)
    y_pairs = _moe_experts(x1, src, dst, te, n_used, w_gate, w_up, w_down, layer, tm_moe, n_tiles)
    return _combine_ln(x1, y_pairs, g1[0], g2[0], gamma, beta, alpha, tm_tok)


def _trunk(x3, pool_prev, mem_k, mem_v, sb_past, p, *, decode):
    b, t, d = x3.shape
    depth = p["w_in"].shape[0]
    n_a = p["w_pool"].shape[0]
    main_w = p["w_sb_kv"].shape[-1] // 2
    mem_w = d - main_w
    n_heads = main_w // HEAD_DIM
    alpha = float((2 * depth) ** 0.25)
    n = b * t
    tm = min(512, n)
    tm_c = min(256, n)
    tm_moe = 256 if n >= 2048 else 16
    pos0 = 0 if not decode else sb_past[2].shape[1] * sb_past[0].shape[1]

    x = x3.reshape(n, d)
    xb = x
    new_pool = []
    k_bf = v_bf = k_f = v_f = None
    for l in range(depth):
        (proj,) = _matmul(xb, p["w_in"], l, [(0, d, F32)], tm, "proj_in")
        proj3 = proj.reshape(b, t, d)
        main = proj3[..., :main_w]
        if l < n_a:
            if decode:
                u_tm = jnp.transpose(main, (1, 0, 2))
                prev_tm = jnp.transpose(pool_prev[l], (1, 0, 2))
                y_tm = _pool_mix_decode(u_tm, prev_tm, p["w_pool"][l], p["pool_scale"][l], pos0)
                y_main = jnp.transpose(y_tm, (1, 0, 2))
            else:
                y_main = _pool_mix(proj3, pool_prev[l], p["w_pool"][l], p["pool_scale"][l],
                                   main_w, min(256, t), pos0)
            if t >= POOL_STATE:
                new_pool.append(main[:, t - POOL_STATE:, :])
            else:
                new_pool.append(jnp.concatenate([pool_prev[l][:, t:, :], main], axis=1))
        else:
            bias = p["sb_bias"][l - n_a]
            if decode:
                y_main = _stick_breaking_decode(main, k_f.reshape(b, t, main_w),
                                                v_f.reshape(b, t, main_w),
                                                sb_past[0], sb_past[1], sb_past[2], bias, pages=8)
            else:
                y_main = _stick_breaking_prompt(proj3, k_bf.reshape(b, t, main_w),
                                                v_bf.reshape(b, t, main_w), bias, n_heads, 256, 6)
        if decode:
            t_pad = 8
            proj_pad = jnp.pad(proj3, ((0, 0), (0, t_pad - t), (0, 0)))
            y_mem = _mem_attend(proj_pad, mem_k[l], mem_v[l], mem_w, t_pad)[:, :t]
        else:
            y_mem = _mem_attend(proj3, mem_k[l], mem_v[l], mem_w, min(512, t))
        x1, lt = _out_ln_route(y_main.reshape(n, main_w), y_mem.reshape(n, mem_w), x,
                               p["w_out"], l, p["ln1_g"][l], p["ln1_b"][l],
                               p["r_cat"][l], p["r_hi"][l], alpha, tm_c)
        x, xb = _hier_moe_ln(x1, lt, p["r_bias"][l], p["w_gate"], p["w_up"], p["w_down"], l,
                             p["ln2_g"][l], p["ln2_b"][l], alpha, tm_moe, tm)
        if l == n_a - 1:
            if decode:
                k_f, v_f = _matmul(xb, p["w_sb_kv"], 0, [(0, main_w, F32), (main_w, main_w, F32)],
                                   n, "proj_sb_kv")
                k_out = k_f.reshape(b, t, n_heads, HEAD_DIM)
                v_out = v_f.reshape(b, t, n_heads, HEAD_DIM)
            else:
                k_hm, v_hm, k_bf, v_bf = _kv_project(xb, p["w_sb_kv"][0], b, t, 256)
                k_out = jnp.transpose(k_hm, (0, 2, 1, 3))
                v_out = jnp.transpose(v_hm, (0, 2, 1, 3))
    return x.reshape(b, t, d), jnp.stack(new_pool, axis=0), k_out, v_out


def kernel(x_prompt, x_sample, state_pool, cache_sb_k, cache_sb_v, cache_mem_k, cache_mem_v,
           page_table, mem_prompt, w_in, w_out, w_pool_grp, pool_scale, w_mem_k, w_mem_v,
           ln1_g, ln1_b, ln2_g, ln2_b, w_route_grp, b_route_grp, w_route_exp, b_route_exp,
           w_gate, w_up, w_down, w_sb_k, w_sb_v, sb_bias):
    depth, d, _ = w_in.shape
    bp, _, _ = x_prompt.shape
    bs, ts, _ = x_sample.shape
    n_a = w_pool_grp.shape[0]
    main_w = w_sb_k.shape[1]
    mem_w = w_mem_k.shape[2]
    n_mem = mem_prompt.shape[1]
    n_heads = main_w // HEAD_DIM
    mem_heads = mem_w // HEAD_DIM

    wr = jnp.concatenate([w_route_grp, w_route_exp.reshape(depth, d, N_EXPERTS)], axis=-1)
    wr = jnp.pad(wr, ((0, 0), (0, 0), (0, LANES - N_GROUPS - N_EXPERTS)))
    r_hi = wr.astype(BF)
    r_lo = (wr - r_hi.astype(F32)).astype(BF)
    r_cat = jnp.concatenate([r_hi, r_lo], axis=-1)
    r_bias = jnp.pad(jnp.concatenate([b_route_grp, b_route_exp.reshape(depth, N_EXPERTS)], axis=-1),
                     ((0, 0), (0, ROUTE_ROWS - N_GROUPS - N_EXPERTS))).reshape(depth, ROUTE_ROWS, 1)
    p = dict(
        w_in=w_in.astype(BF), w_out=w_out.astype(BF), w_pool=w_pool_grp.astype(BF),
        pool_scale=pool_scale, w_sb_kv=jnp.concatenate([w_sb_k, w_sb_v], axis=1).astype(BF)[None],
        ln1_g=ln1_g, ln1_b=ln1_b, ln2_g=ln2_g, ln2_b=ln2_b, r_hi=r_hi, r_cat=r_cat, r_bias=r_bias,
        w_gate=w_gate, w_up=w_up, w_down=w_down, sb_bias=sb_bias)

    w_mem = jnp.concatenate([jnp.transpose(w_mem_k, (1, 0, 2)).reshape(d, depth * mem_w),
                             jnp.transpose(w_mem_v, (1, 0, 2)).reshape(d, depth * mem_w)],
                            axis=1).astype(BF)
    mem_kv = _mem_project(mem_prompt.reshape(bp * n_mem, d).astype(BF), w_mem, mem_w)
    mem_kv = mem_kv.reshape(2, depth, bp, n_mem, mem_w)
    pool0 = jnp.zeros((n_a, bp, POOL_STATE, main_w), x_prompt.dtype)
    y_p, pool_p, k_p, v_p = _trunk(x_prompt, pool0, mem_kv[0], mem_kv[1], None, p, decode=False)

    sb_past = (cache_sb_k, cache_sb_v, page_table)
    y_s, pool_s, k_s, v_s = _trunk(
        x_sample, state_pool, cache_mem_k.reshape(depth, bs, n_mem, mem_w),
        cache_mem_v.reshape(depth, bs, n_mem, mem_w), sb_past, p, decode=True)

    return (y_p, y_s, pool_p, pool_s, k_p, v_p, k_s, v_s,
            mem_kv[0].reshape(depth, bp, n_mem, mem_heads, HEAD_DIM),
            mem_kv[1].reshape(depth, bp, n_mem, mem_heads, HEAD_DIM))
```

```python
import functools

import jax
import jax.numpy as jnp
import numpy as np
from jax import lax
from jax.experimental import pallas as pl
from jax.experimental.pallas import tpu as pltpu

F32 = jnp.float32
BF = jnp.bfloat16
I32 = jnp.int32

HEAD_DIM = 128
LANES = 128
POOL_WINDOWS = (2, 4, 8, 16)
POOL_STATE = max(POOL_WINDOWS) - 1
N_GROUPS = 4
EXP_PER_GROUP = 4
N_EXPERTS = N_GROUPS * EXP_PER_GROUP
LN_EPS = 1e-5
ROUTE_ROWS = 32
VMEM_LIMIT = 56 * 2**20

_NT = (((1,), (1,)), ((), ()))


def _params(sem):
    return pltpu.CompilerParams(dimension_semantics=sem, vmem_limit_bytes=VMEM_LIMIT)


def _dot(a, b):
    return jnp.dot(a, b, preferred_element_type=F32)


def _layer_norm(v, g, b):
    mu = jnp.mean(v, axis=-1, keepdims=True)
    c = v - mu
    var = jnp.mean(c * c, axis=-1, keepdims=True)
    return c * lax.rsqrt(var + LN_EPS) * g + b


def _softplus(z):
    return jnp.maximum(z, 0.0) + jnp.log(1.0 + jnp.exp(jnp.minimum(z, -z)))


def _split_bf16(x):
    hi = x.astype(BF)
    lo = (x - hi.astype(F32)).astype(BF)
    return hi, lo


def _mm_kernel(x_ref, w_ref, *o_refs, offsets):
    acc = _dot(x_ref[...].astype(BF), w_ref[...])
    for o_ref, off in zip(o_refs, offsets):
        o_ref[...] = acc[:, off:off + o_ref.shape[-1]].astype(o_ref.dtype)


def _matmul(x, w, layer, outs, tm, name):
    m, k = x.shape
    n = w.shape[-1]
    return pl.pallas_call(
        functools.partial(_mm_kernel, offsets=tuple(o for o, _, _ in outs)),
        grid=(m // tm,),
        in_specs=[pl.BlockSpec((tm, k), lambda i: (i, 0)),
                  pl.BlockSpec((None, k, n), lambda i: (layer, 0, 0))],
        out_specs=[pl.BlockSpec((tm, c), lambda i: (i, 0)) for _, c, _ in outs],
        out_shape=[jax.ShapeDtypeStruct((m, c), dt) for _, c, dt in outs],
        compiler_params=_params(("parallel",)),
        name=name,
    )(x, w)


def _kv_kernel(x_ref, w_ref, kf_ref, vf_ref, kb_ref, vb_ref):
    n_heads = kf_ref.shape[1]
    main_w = n_heads * HEAD_DIM
    acc = _dot(x_ref[...], w_ref[...])
    for h in range(n_heads):
        kf_ref[0, h] = acc[:, h * HEAD_DIM:(h + 1) * HEAD_DIM]
        vf_ref[0, h] = acc[:, main_w + h * HEAD_DIM:main_w + (h + 1) * HEAD_DIM]
    kb_ref[...] = acc[:, :main_w].astype(BF)
    vb_ref[...] = acc[:, main_w:].astype(BF)


def _kv_project(xb, w_kv, b, t, tm):
    n, d = xb.shape
    main_w = w_kv.shape[1] // 2
    n_heads = main_w // HEAD_DIM
    nt = t // tm
    head_major = pl.BlockSpec((1, n_heads, tm, HEAD_DIM), lambda i: (i // nt, 0, i % nt, 0))
    return pl.pallas_call(
        _kv_kernel,
        grid=(n // tm,),
        in_specs=[pl.BlockSpec((tm, d), lambda i: (i, 0)),
                  pl.BlockSpec((d, 2 * main_w), lambda i: (0, 0))],
        out_specs=[head_major, head_major,
                   pl.BlockSpec((tm, main_w), lambda i: (i, 0)),
                   pl.BlockSpec((tm, main_w), lambda i: (i, 0))],
        out_shape=[jax.ShapeDtypeStruct((b, n_heads, t, HEAD_DIM), F32)] * 2
                  + [jax.ShapeDtypeStruct((n, main_w), BF)] * 2,
        compiler_params=_params(("parallel",)),
        name="proj_sb_kv",
    )(xb, w_kv)


def _memproj_kernel(x_ref, w_ref, o_ref):
    o_ref[0] = _dot(x_ref[...], w_ref[...])


def _mem_project(mem_bf, w_cat, width):
    m, d = mem_bf.shape
    nj = w_cat.shape[1] // width
    return pl.pallas_call(
        _memproj_kernel,
        grid=(nj,),
        in_specs=[pl.BlockSpec((m, d), lambda j: (0, 0)),
                  pl.BlockSpec((d, width), lambda j: (0, j))],
        out_specs=pl.BlockSpec((1, m, width), lambda j: (j, 0, 0)),
        out_shape=jax.ShapeDtypeStruct((nj, m, width), F32),
        compiler_params=_params(("parallel",)),
        name="mem_project",
    )(mem_bf, w_cat)


def _pool_kernel(u_ref, prev_ref, w_ref, sc_ref, y_ref, buf, *, tt, nt, pos0):
    t = pl.program_id(1)
    p = POOL_STATE
    cg = w_ref.shape[-1]

    @pl.when(t == 0)
    def _():
        buf[1:1 + p, :] = prev_ref[0]

    u = u_ref[0]
    buf[p + 1:p + 1 + tt, :] = u
    pos = lax.broadcasted_iota(I32, (tt, 1), 0) + (t * tt + pos0)
    for g, w in enumerate(POOL_WINDOWS):
        c0 = g * cg
        ug = u[:, c0:c0 + cg]
        s = ug
        for k in range(1, w):
            s = s + buf[p + 1 - k:p + 1 - k + tt, c0:c0 + cg]
        cnt = jnp.minimum(w, pos + 1).astype(F32)
        d = (s / cnt - ug).astype(BF)
        yg = _dot(d, w_ref[g]) * sc_ref[:, c0:c0 + cg]
        y_ref[0, :, c0:c0 + cg] = yg.astype(y_ref.dtype)
    if nt > 1:
        buf[1:1 + p, :] = buf[tt + 1:tt + 1 + p, :]


def _pool_mix(proj, prev, w_grp_bf, scale, main_w, tt, pos0):
    b, t, _ = proj.shape
    nt = t // tt
    cg = main_w // len(POOL_WINDOWS)
    return pl.pallas_call(
        functools.partial(_pool_kernel, tt=tt, nt=nt, pos0=pos0),
        grid=(b, nt),
        in_specs=[pl.BlockSpec((1, tt, main_w), lambda i, j: (i, j, 0)),
                  pl.BlockSpec((1, POOL_STATE, main_w), lambda i, j: (i, 0, 0)),
                  pl.BlockSpec((len(POOL_WINDOWS), cg, cg), lambda i, j: (0, 0, 0)),
                  pl.BlockSpec((1, main_w), lambda i, j: (0, 0))],
        out_specs=pl.BlockSpec((1, tt, main_w), lambda i, j: (i, j, 0)),
        out_shape=jax.ShapeDtypeStruct((b, t, main_w), BF),
        scratch_shapes=[pltpu.VMEM((POOL_STATE + 1 + tt, main_w), F32)],
        compiler_params=_params(("parallel", "arbitrary")),
        name="pool_mix",
    )(proj, prev, w_grp_bf, scale.reshape(1, main_w))


def _pool_dec_kernel(u_ref, prev_ref, w_ref, sc_ref, y_ref, *, pos0):
    p = POOL_STATE
    nt = u_ref.shape[0]
    cg = w_ref.shape[-1]
    cat = [prev_ref[k] for k in range(p)] + [u_ref[i] for i in range(nt)]
    for g, w in enumerate(POOL_WINDOWS):
        c0 = g * cg
        ds = []
        for i in range(nt):
            s = cat[p + i][:, c0:c0 + cg]
            for k in range(1, w):
                s = s + cat[p + i - k][:, c0:c0 + cg]
            cnt = float(min(w, pos0 + i + 1))
            ds.append(s / cnt - cat[p + i][:, c0:c0 + cg])
        d = jnp.concatenate(ds, axis=0).astype(BF)
        yg = _dot(d, w_ref[g]) * sc_ref[:, c0:c0 + cg]
        nb = ds[0].shape[0]
        for i in range(nt):
            y_ref[i, :, c0:c0 + cg] = yg[i * nb:(i + 1) * nb].astype(y_ref.dtype)


def _pool_mix_decode(u_tm, prev_tm, w_grp_bf, scale, pos0):
    nt, b, c = u_tm.shape
    return pl.pallas_call(
        functools.partial(_pool_dec_kernel, pos0=pos0),
        out_shape=jax.ShapeDtypeStruct((nt, b, c), BF),
        compiler_params=pltpu.CompilerParams(vmem_limit_bytes=VMEM_LIMIT),
        name="pool_mix_decode",
    )(u_tm, prev_tm, w_grp_bf, scale.reshape(1, c))


def _mem_kernel(q_ref, k_ref, v_ref, o_ref, *, cdt):
    nh = q_ref.shape[-1] // HEAD_DIM
    scale = HEAD_DIM ** -0.5
    for h in range(nh):
        sl = slice(h * HEAD_DIM, (h + 1) * HEAD_DIM)
        q = q_ref[0, :, sl].astype(cdt)
        k = k_ref[0, :, sl].astype(cdt)
        v = v_ref[0, :, sl].astype(cdt)
        s = lax.dot_general(q, k, _NT, preferred_element_type=F32) * scale
        e = jnp.exp(s - jnp.max(s, axis=-1, keepdims=True))
        den = jnp.sum(e, axis=-1, keepdims=True)
        o = _dot(e.astype(cdt), v) / den
        o_ref[0, :, sl] = o.astype(o_ref.dtype)


def _mem_attend(proj, mk, mv, mem_w, tt):
    b, t, mix_w = proj.shape
    n_mem = mk.shape[1]
    qblk = (mix_w - mem_w) // mem_w
    cdt = BF if tt % 16 == 0 else F32
    return pl.pallas_call(
        functools.partial(_mem_kernel, cdt=cdt),
        grid=(b, t // tt),
        in_specs=[pl.BlockSpec((1, tt, mem_w), lambda i, j: (i, j, qblk)),
                  pl.BlockSpec((1, n_mem, mem_w), lambda i, j: (i, 0, 0)),
                  pl.BlockSpec((1, n_mem, mem_w), lambda i, j: (i, 0, 0))],
        out_specs=pl.BlockSpec((1, tt, mem_w), lambda i, j: (i, j, 0)),
        out_shape=jax.ShapeDtypeStruct((b, t, mem_w), BF),
        compiler_params=_params(("parallel", "parallel")),
        name="mem_attend",
    )(proj, mk, mv)


def _sb_kernel(q_ref, k_ref, v_ref, bias_ref, o_ref, *, tq, hp):
    qi = pl.program_id(2)
    inv_sqrt = 1.0 / np.sqrt(np.float32(HEAD_DIM))
    jj = lax.broadcasted_iota(I32, (tq, tq), 0)
    ss = lax.broadcasted_iota(I32, (tq, tq), 1)
    later = (jj > ss).astype(BF)
    later2 = jnp.concatenate([later, later], axis=0)
    causal = ss < jj
    heads = [slice(h * HEAD_DIM, (h + 1) * HEAD_DIM) for h in range(hp)]
    qs = [(q_ref[0, :, sl] * inv_sqrt).astype(BF) for sl in heads]
    biases = [bias_ref[0, h:h + 1, :] for h in range(hp)]

    def blocks(start, state, mask):
        rng = range(hp)
        zs = [lax.dot_general(qs[h], k_ref[0, pl.ds(start, tq), heads[h]], _NT,
                              preferred_element_type=F32) + biases[h] for h in rng]
        sps = [_softplus(z) for z in zs]
        if mask is not None:
            sps = [jnp.where(mask, sp, 0.0) for sp in sps]
        splits = [_split_bf16(sp) for sp in sps]
        later_sums = [_dot(jnp.concatenate([hi, lo], axis=1), later2) for hi, lo in splits]
        ws = [jnp.exp(zs[h] - sps[h] - later_sums[h] + state[h][0]) for h in rng]
        if mask is not None:
            ws = [jnp.where(mask, a, 0.0) for a in ws]
        accs = [state[h][1] + _dot(ws[h].astype(BF), v_ref[0, pl.ds(start, tq), heads[h]])
                for h in rng]
        tails = [state[h][0] - jnp.sum(sps[h], axis=1, keepdims=True) for h in rng]
        return tuple(zip(tails, accs))

    diag = pl.multiple_of(qi * tq, tq)
    zero = (jnp.zeros((tq, 1), F32), jnp.zeros((tq, HEAD_DIM), F32))
    state = blocks(diag, (zero,) * hp, causal)

    def body(n, carry):
        return blocks(pl.multiple_of((qi - 1 - n) * tq, tq), carry, None)

    state = lax.fori_loop(0, qi, body, state)
    for h in range(hp):
        o_ref[0, :, heads[h]] = state[h][1].astype(o_ref.dtype)


def _stick_breaking_prompt(proj, k_bf, v_bf, bias, n_heads, tq, hp):
    b, t, _ = proj.shape
    main_w = n_heads * HEAD_DIM
    w = hp * HEAD_DIM
    bias_b = jnp.broadcast_to(bias.astype(F32).reshape(n_heads // hp, hp, 1),
                              (n_heads // hp, hp, tq))
    return pl.pallas_call(
        functools.partial(_sb_kernel, tq=tq, hp=hp),
        grid=(b, n_heads // hp, t // tq),
        in_specs=[pl.BlockSpec((1, tq, w), lambda i, h, j: (i, j, h)),
                  pl.BlockSpec((1, t, w), lambda i, h, j: (i, 0, h)),
                  pl.BlockSpec((1, t, w), lambda i, h, j: (i, 0, h)),
                  pl.BlockSpec((1, hp, tq), lambda i, h, j: (h, 0, 0))],
        out_specs=pl.BlockSpec((1, tq, w), lambda i, h, j: (i, j, h)),
        out_shape=jax.ShapeDtypeStruct((b, t, main_w), BF),
        compiler_params=_params(("parallel", "parallel", "arbitrary")),
        name="stick_breaking_prompt",
    )(proj, k_bf, v_bf, bias_b)


def _sb_dec_kernel(pt_ref, qt_ref, bias_ref, kn_ref, vn_ref, *rest, pages, n_q, n_new):
    k_refs = rest[:pages]
    v_refs = rest[pages:2 * pages]
    o_ref, tail_ref, acc_ref = rest[2 * pages:]
    j = pl.program_id(1)
    inv_sqrt = 1.0 / np.sqrt(np.float32(HEAD_DIM))
    qt = qt_ref[0]
    bias = bias_ref[...]
    n_cols = acc_ref.shape[0]
    nk = kn_ref.shape[1]
    n_heads = k_refs[0].shape[1]
    ss = lax.broadcasted_iota(I32, (nk, nk), 0)
    jj = lax.broadcasted_iota(I32, (nk, nk), 1)
    later = (jj > ss).astype(BF)

    def heads_cat(ref):
        return jnp.concatenate([ref[0, h] for h in range(n_heads)], axis=1).astype(BF)

    def scores(kcat, mask):
        z = _dot(kcat, qt) * inv_sqrt + bias
        sp = _softplus(z)
        if mask is not None:
            sp = jnp.where(mask, sp, 0.0)
        hi, lo = _split_bf16(sp)
        both = _dot(later, jnp.concatenate([hi, lo], axis=1))
        later_sum = both[:, :128] + both[:, 128:]
        return z, sp, later_sum, jnp.sum(sp, axis=0, keepdims=True)

    def weights(z, sp, later_sum, tail, mask):
        a = jnp.exp(z - sp - later_sum + tail)
        if mask is not None:
            a = jnp.where(mask, a, 0.0)
        return a.T[:n_cols].astype(BF)

    @pl.when(j == 0)
    def _():
        r = lax.broadcasted_iota(I32, (nk, 128), 0)
        c = lax.broadcasted_iota(I32, (nk, 128), 1)
        mask = (r < c % n_q) & (r < n_new)
        z, sp, later_sum, total = scores(kn_ref[0].astype(BF), mask)
        a_t = weights(z, sp, later_sum, jnp.zeros_like(total), mask)
        acc_ref[...] = _dot(a_t, vn_ref[0].astype(BF))
        tail_ref[...] = -total

    parts = [scores(heads_cat(k_refs[g]), None) for g in range(pages)]
    tail = tail_ref[...]
    a_ts = []
    for z, sp, later_sum, total in parts:
        a_ts.append(weights(z, sp, later_sum, tail, None))
        tail = tail - total
    v_all = jnp.concatenate([heads_cat(v_refs[g]) for g in range(pages)], axis=0)
    acc_ref[...] += _dot(jnp.concatenate(a_ts, axis=1), v_all)
    tail_ref[...] = tail

    @pl.when(j == pl.num_programs(1) - 1)
    def _():
        for h in range(o_ref.shape[-1] // HEAD_DIM):
            sl = slice(h * HEAD_DIM, (h + 1) * HEAD_DIM)
            o_ref[0, :, sl] = acc_ref[h * n_q:(h + 1) * n_q, sl].astype(o_ref.dtype)


def _stick_breaking_decode(q, k_new, v_new, cache_k, cache_v, page_table, bias, pages):
    b, t, main_w = q.shape
    n_pages = page_table.shape[1]
    _, page, n_heads, _ = cache_k.shape
    n_cols = n_heads * t
    qh = q.reshape(b, t, n_heads, HEAD_DIM).astype(F32)
    eye = jnp.eye(n_heads, dtype=F32)
    qt = jnp.einsum("bihd,hg->bhdgi", qh, eye).reshape(b, main_w, n_cols)
    qt = jnp.pad(qt, ((0, 0), (0, 0), (0, 128 - n_cols))).astype(BF)
    bias_c = jnp.pad(jnp.repeat(bias.astype(F32), t), (0, 128 - n_cols)).reshape(1, 128)
    new_rows = page
    k_pad = jnp.pad(k_new, ((0, 0), (0, new_rows - t), (0, 0)))
    v_pad = jnp.pad(v_new, ((0, 0), (0, new_rows - t), (0, 0)))

    def page_map(g):
        return lambda i, j, pt: (pt[i, n_pages - 1 - (j * pages + g)], 0, 0, 0)

    cache_k = jnp.transpose(cache_k, (0, 2, 1, 3))
    cache_v = jnp.transpose(cache_v, (0, 2, 1, 3))
    cache_specs = [pl.BlockSpec((1, n_heads, page, HEAD_DIM), page_map(g)) for g in range(pages)]
    grid_spec = pltpu.PrefetchScalarGridSpec(
        num_scalar_prefetch=1,
        grid=(b, n_pages // pages),
        in_specs=[pl.BlockSpec((1, main_w, 128), lambda i, j, pt: (i, 0, 0)),
                  pl.BlockSpec((1, 128), lambda i, j, pt: (0, 0)),
                  pl.BlockSpec((1, new_rows, main_w), lambda i, j, pt: (i, 0, 0)),
                  pl.BlockSpec((1, new_rows, main_w), lambda i, j, pt: (i, 0, 0))]
                 + cache_specs + cache_specs,
        out_specs=pl.BlockSpec((1, t, main_w), lambda i, j, pt: (i, 0, 0)),
        scratch_shapes=[pltpu.VMEM((1, 128), F32), pltpu.VMEM((n_cols, main_w), F32)],
    )
    return pl.pallas_call(
        functools.partial(_sb_dec_kernel, pages=pages, n_q=t, n_new=t),
        grid_spec=grid_spec,
        out_shape=jax.ShapeDtypeStruct((b, t, main_w), BF),
        compiler_params=_params(("parallel", "arbitrary")),
        name="stick_breaking_decode",
    )(page_table, qt, bias_c, k_pad, v_pad, *([cache_k] * pages), *([cache_v] * pages))


def _out_ln_kernel(ym_ref, ye_ref, x_ref, wa_ref, wb_ref, g_ref, b_ref, rc_ref, rh_ref,
                   x1_ref, lt_ref, hbuf, *, alpha):
    @pl.when(pl.program_id(0) == 0)
    def _():
        hbuf[...] = jnp.zeros_like(hbuf)

    tm, d = x_ref.shape
    chunks = d // LANES
    x1 = _layer_norm(alpha * x_ref[...] + hbuf[...], g_ref[...], b_ref[...])
    for c in range(chunks):
        x1_ref[pl.ds(c, tm, stride=chunks), :] = x1[:, c * LANES:(c + 1) * LANES]
    xh, xl = _split_bf16(x1)
    both = _dot(xh, rc_ref[...])
    lt = both[:, :LANES] + both[:, LANES:] + _dot(xl, rh_ref[...])
    lt_ref[...] = lt.T[:ROUTE_ROWS]
    hbuf[...] = _dot(ym_ref[...], wa_ref[...]) + _dot(ye_ref[...], wb_ref[...])


def _out_ln_route(y_main, y_mem, x, w_out_bf, layer, gamma, beta, r_cat, r_hi, alpha, tm):
    n, d = x.shape
    main_w = y_main.shape[1]
    mem_w = y_mem.shape[1]
    nt = n // tm
    chunks = d // LANES
    cur = lambda i: (jnp.minimum(i, nt - 1), 0)
    prev = lambda i: (jnp.maximum(i - 1, 0), 0)
    return pl.pallas_call(
        functools.partial(_out_ln_kernel, alpha=alpha),
        grid=(nt + 1,),
        in_specs=[pl.BlockSpec((tm, main_w), cur),
                  pl.BlockSpec((tm, mem_w), cur),
                  pl.BlockSpec((tm, d), prev),
                  pl.BlockSpec((None, main_w, d), lambda i: (layer, 0, 0)),
                  pl.BlockSpec((None, mem_w, d), lambda i: (layer, main_w // mem_w, 0)),
                  pl.BlockSpec((1, d), lambda i: (0, 0)),
                  pl.BlockSpec((1, d), lambda i: (0, 0)),
                  pl.BlockSpec((d, 2 * LANES), lambda i: (0, 0)),
                  pl.BlockSpec((d, LANES), lambda i: (0, 0))],
        out_specs=[pl.BlockSpec((tm * chunks, LANES), prev),
                   pl.BlockSpec((ROUTE_ROWS, tm), lambda i: (0, jnp.maximum(i - 1, 0)))],
        out_shape=[jax.ShapeDtypeStruct((n * chunks, LANES), F32),
                   jax.ShapeDtypeStruct((ROUTE_ROWS, n), F32)],
        scratch_shapes=[pltpu.VMEM((tm, d), F32)],
        compiler_params=_params(("arbitrary",)),
        name="out_ln_route",
    )(y_main, y_mem, x, w_out_bf, w_out_bf, gamma.reshape(1, d), beta.reshape(1, d), r_cat, r_hi)


def _route_kernel(lt_ref, rb_ref, e1_ref, e2_ref, g1_ref, g2_ref):
    lt = lt_ref[...] + rb_ref[...]
    ng, ne = N_GROUPS, EXP_PER_GROUP
    lg = [lt[g:g + 1, :] for g in range(ng)]
    m = lg[0]
    gi = jnp.zeros(m.shape, I32)
    for g in range(1, ng):
        better = lg[g] > m
        gi = jnp.where(better, g, gi)
        m = jnp.where(better, lg[g], m)
    den = jnp.exp(lg[0] - m)
    for g in range(1, ng):
        den = den + jnp.exp(lg[g] - m)
    p_sel = 1.0 / den
    le = []
    for e in range(ne):
        v = lt[ng + e:ng + e + 1, :]
        for g in range(1, ng):
            r = ng + g * ne + e
            v = jnp.where(gi == g, lt[r:r + 1, :], v)
        le.append(v)
    v1 = le[0]
    i1 = jnp.zeros(m.shape, I32)
    for e in range(1, ne):
        better = le[e] > v1
        i1 = jnp.where(better, e, i1)
        v1 = jnp.where(better, le[e], v1)
    v2 = jnp.full(m.shape, -jnp.inf, F32)
    i2 = jnp.zeros(m.shape, I32)
    for e in range(ne):
        better = (i1 != e) & (le[e] > v2)
        i2 = jnp.where(better, e, i2)
        v2 = jnp.where(better, le[e], v2)
    t = jnp.exp(v2 - v1)
    w_a = 1.0 / (1.0 + t)
    w_b = t / (1.0 + t)
    e1_ref[...] = gi * ne + i1
    e2_ref[...] = gi * ne + i2
    g1_ref[...] = p_sel * w_a
    g2_ref[...] = p_sel * w_b


def _route(lt, rbias):
    n = lt.shape[1]
    shp = lambda dt: jax.ShapeDtypeStruct((1, n), dt)
    return pl.pallas_call(
        _route_kernel,
        out_shape=[shp(I32), shp(I32), shp(F32), shp(F32)],
        compiler_params=pltpu.CompilerParams(vmem_limit_bytes=VMEM_LIMIT),
        name="route",
    )(lt, rbias)


def _moe_plan(e1, e2, tm, n_tiles):
    n = e1.shape[0]
    e = jnp.concatenate([e1, e2])
    oh = (e[:, None] == jnp.arange(N_EXPERTS, dtype=I32)[None, :]).astype(I32)
    csum = jnp.cumsum(oh, axis=0)
    rank = jnp.sum((csum - oh) * oh, axis=1)
    counts = csum[-1]
    tiles = (counts + tm - 1) // tm
    tile_end = jnp.cumsum(tiles)
    starts = (tile_end - tiles) * tm
    dest = jnp.sum(oh * starts[None, :], axis=1) + rank
    n_used = tile_end[-1]
    tidx = jnp.arange(n_tiles, dtype=I32)
    te = jnp.sum((tidx[:, None] >= tile_end[None, :]).astype(I32), axis=1)
    last_e = jnp.max(jnp.where(tiles > 0, jnp.arange(N_EXPERTS, dtype=I32), 0))
    te = jnp.where(tidx < n_used, te, last_e).astype(I32)
    rows = n_tiles * tm
    pair_of_row = jnp.full((rows,), -1, I32).at[dest].set(jnp.arange(2 * n, dtype=I32))
    has_pair = pair_of_row >= 0
    filler = jnp.logical_and(~has_pair, jnp.arange(rows, dtype=I32) < n_used * tm)
    spill = 2 * n + tm + jnp.cumsum(filler.astype(I32)) - 1
    src = jnp.where(has_pair, pair_of_row % n, 0).astype(I32)
    dst_rows = jnp.where(has_pair, pair_of_row, jnp.where(filler, spill, 2 * n))
    dst = jnp.concatenate([2 * n + jnp.arange(tm, dtype=I32), dst_rows]).astype(I32)
    return src, dst, te, n_used.reshape(1).astype(I32)


def _moe_kernel(te_ref, src_ref, dst_ref, nu_ref, x_hbm, wg_ref, wu_ref, wd_ref, y_hbm,
                xbuf, ybuf, gsem, ssem, wg_s, wu_s, wd_s, *, tm, n_pairs, n_spill_tiles):
    j = pl.program_id(0)
    n_used = nu_ref[0]
    slot = j % 3
    ahead = (j + 2) % 3
    chunks = wg_s.shape[0] // LANES
    tile_rows = tm * chunks

    def token_rows(token, rows):
        return pl.ds(pl.multiple_of(token * rows, rows), rows)

    def start_gather(tile, to_slot):
        for r in range(tm):
            pltpu.make_async_copy(x_hbm.at[token_rows(src_ref[tile * tm + r], chunks)],
                                  xbuf.at[token_rows(to_slot * tm + r, chunks)],
                                  gsem.at[to_slot]).start()

    def wait_gather(of_slot):
        pltpu.make_async_copy(x_hbm.at[pl.ds(0, tile_rows)], xbuf.at[token_rows(of_slot, tile_rows)],
                              gsem.at[of_slot]).wait()

    def start_scatter(tile, from_slot):
        for r in range(tm):
            pltpu.make_async_copy(ybuf.at[token_rows(from_slot * tm + r, chunks)],
                                  y_hbm.at[token_rows(dst_ref[(tile + 1) * tm + r], chunks)],
                                  ssem.at[from_slot]).start()

    def wait_scatter(of_slot):
        pltpu.make_async_copy(ybuf.at[token_rows(of_slot, tile_rows)], y_hbm.at[pl.ds(0, tile_rows)],
                              ssem.at[of_slot]).wait()

    @pl.when(j == 0)
    def _():
        start_gather(0, 0)
        start_gather(1, 1)
        ybuf[pl.ds(2 * tile_rows, tile_rows), :] = jnp.zeros((tile_rows, LANES), F32)
        fills = [pltpu.make_async_copy(ybuf.at[pl.ds(2 * tile_rows, tile_rows)],
                                       y_hbm.at[pl.ds((n_pairs + c * tm) * chunks, tile_rows)],
                                       ssem.at[2]) for c in range(1, n_spill_tiles)]
        for f in fills:
            f.start()
        for f in fills:
            f.wait()

    e = te_ref[j]
    e_prev = te_ref[jnp.maximum(j - 1, 0)]

    @pl.when((j == 0) | (e != e_prev))
    def _():
        wg_s[...] = wg_ref[0, 0].astype(BF)
        wu_s[...] = wu_ref[0, 0].astype(BF)
        wd_s[...] = wd_ref[0, 0].astype(BF)

    @pl.when((j >= 2) & (j <= n_used + 2))
    def _():
        wait_scatter(slot)

    @pl.when(j < n_used)
    def _():
        wait_gather(slot)
        start_gather(j + 2, ahead)
        start_scatter(j - 1, ahead)
        base = slot * tile_rows
        x = jnp.concatenate([xbuf[pl.ds(base + c, tm, stride=chunks), :] for c in range(chunks)],
                            axis=1).astype(BF)
        g = _dot(x, wg_s[...])
        u = _dot(x, wu_s[...])
        h = (g * jax.nn.sigmoid(g) * u).astype(BF)
        y = _dot(h, wd_s[...])
        for c in range(chunks):
            ybuf[pl.ds(base + c, tm, stride=chunks), :] = y[:, c * LANES:(c + 1) * LANES]

    @pl.when(j == n_used)
    def _():
        wait_gather(slot)
        start_scatter(j - 1, ahead)

    @pl.when(j == n_used + 1)
    def _():
        wait_gather(slot)


def _moe_experts(x1, src, dst, te, n_used, w_gate, w_up, w_down, layer, tm, n_tiles):
    d, ff = w_gate.shape[-2:]
    chunks = d // LANES
    n = x1.shape[0] // chunks
    n_spill_tiles = 1 + -(-N_EXPERTS * (tm - 1) // tm)
    grid_spec = pltpu.PrefetchScalarGridSpec(
        num_scalar_prefetch=4,
        grid=(n_tiles,),
        in_specs=[pl.BlockSpec(memory_space=pl.ANY),
                  pl.BlockSpec((1, 1, d, ff), lambda j, te, *_: (layer, te[j], 0, 0)),
                  pl.BlockSpec((1, 1, d, ff), lambda j, te, *_: (layer, te[j], 0, 0)),
                  pl.BlockSpec((1, 1, ff, d), lambda j, te, *_: (layer, te[j], 0, 0))],
        out_specs=pl.BlockSpec(memory_space=pl.ANY),
        scratch_shapes=[pltpu.VMEM((3 * tm * chunks, LANES), F32),
                        pltpu.VMEM((3 * tm * chunks, LANES), F32),
                        pltpu.SemaphoreType.DMA((3,)),
                        pltpu.SemaphoreType.DMA((3,)),
                        pltpu.VMEM((d, ff), BF),
                        pltpu.VMEM((d, ff), BF),
                        pltpu.VMEM((ff, d), BF)],
    )
    return pl.pallas_call(
        functools.partial(_moe_kernel, tm=tm, n_pairs=2 * n, n_spill_tiles=n_spill_tiles),
        grid_spec=grid_spec,
        out_shape=jax.ShapeDtypeStruct(((2 * n + n_spill_tiles * tm) * chunks, LANES), F32),
        compiler_params=_params(("arbitrary",)),
        name="moe_experts",
    )(te, src, dst, n_used, x1, w_gate, w_up, w_down)


def _combine_kernel(x1_ref, ya_ref, yb_ref, g1_ref, g2_ref, gam_ref, bet_ref,
                    x2_ref, x2b_ref, *, alpha):
    tm, d = x2_ref.shape
    chunks = d // LANES
    g1 = g1_ref[...]
    g2 = g2_ref[...]
    cols = []
    for c in range(chunks):
        rows = pl.ds(c, tm, stride=chunks)
        cols.append(alpha * x1_ref[rows, :] + g1 * ya_ref[rows, :] + g2 * yb_ref[rows, :])
    x2 = _layer_norm(jnp.concatenate(cols, axis=1), gam_ref[...], bet_ref[...])
    x2_ref[...] = x2
    x2b_ref[...] = x2.astype(BF)


def _combine_ln(x1, y_pairs, g1, g2, gamma, beta, alpha, tm):
    d = gamma.shape[0]
    chunks = d // LANES
    n = x1.shape[0] // chunks
    nb = n // tm
    return pl.pallas_call(
        functools.partial(_combine_kernel, alpha=alpha),
        grid=(nb,),
        in_specs=[pl.BlockSpec((tm * chunks, LANES), lambda i: (i, 0)),
                  pl.BlockSpec((tm * chunks, LANES), lambda i: (i, 0)),
                  pl.BlockSpec((tm * chunks, LANES), lambda i: (i + nb, 0)),
                  pl.BlockSpec((tm, 1), lambda i: (i, 0)),
                  pl.BlockSpec((tm, 1), lambda i: (i, 0)),
                  pl.BlockSpec((1, d), lambda i: (0, 0)),
                  pl.BlockSpec((1, d), lambda i: (0, 0))],
        out_specs=[pl.BlockSpec((tm, d), lambda i: (i, 0)),
                   pl.BlockSpec((tm, d), lambda i: (i, 0))],
        out_shape=[jax.ShapeDtypeStruct((n, d), F32), jax.ShapeDtypeStruct((n, d), BF)],
        compiler_params=_params(("parallel",)),
        name="combine_ln",
    )(x1, y_pairs, y_pairs, g1.reshape(n, 1), g2.reshape(n, 1), gamma.reshape(1, d),
      beta.reshape(1, d))


def _hier_moe_ln(x1, lt, rbias, w_gate, w_up, w_down, layer, gamma, beta, alpha, tm_moe, tm_tok):
    n = lt.shape[1]
    e1, e2, g1, g2 = _route(lt, rbias)
    n_tiles = (2 * n + N_EXPERTS * (tm_moe - 1)) // tm_moe + 3
    src, dst, te, n_used = _moe_plan(e1[0], e2[0], tm_moe, n_tiles)
    y_pairs = _moe_experts(x1, src, dst, te, n_used, w_gate, w_up, w_down, layer, tm_moe, n_tiles)
    return _combine_ln(x1, y_pairs, g1[0], g2[0], gamma, beta, alpha, tm_tok)


def _trunk(x3, pool_prev, mem_k, mem_v, sb_past, p, *, decode):
    b, t, d = x3.shape
    depth = p["w_in"].shape[0]
    n_a = p["w_pool"].shape[0]
    main_w = p["w_sb_kv"].shape[-1] // 2
    mem_w = d - main_w
    n_heads = main_w // HEAD_DIM
    alpha = float((2 * depth) ** 0.25)
    n = b * t
    tm = min(512, n)
    tm_c = min(256, n)
    tm_moe = 256 if n >= 2048 else 16
    pos0 = 0 if not decode else sb_past[2].shape[1] * sb_past[0].shape[1]

    x = x3.reshape(n, d)
    xb = x
    new_pool = []
    k_bf = v_bf = k_f = v_f = None
    for l in range(depth):
        (proj,) = _matmul(xb, p["w_in"], l, [(0, d, F32)], tm, "proj_in")
        proj3 = proj.reshape(b, t, d)
        main = proj3[..., :main_w]
        if l < n_a:
            if decode:
                u_tm = jnp.transpose(main, (1, 0, 2))
                prev_tm = jnp.transpose(pool_prev[l], (1, 0, 2))
                y_tm = _pool_mix_decode(u_tm, prev_tm, p["w_pool"][l], p["pool_scale"][l], pos0)
                y_main = jnp.transpose(y_tm, (1, 0, 2))
            else:
                y_main = _pool_mix(proj3, pool_prev[l], p["w_pool"][l], p["pool_scale"][l],
                                   main_w, min(256, t), pos0)
            if t >= POOL_STATE:
                new_pool.append(main[:, t - POOL_STATE:, :])
            else:
                new_pool.append(jnp.concatenate([pool_prev[l][:, t:, :], main], axis=1))
        else:
            bias = p["sb_bias"][l - n_a]
            if decode:
                y_main = _stick_breaking_decode(main, k_f.reshape(b, t, main_w),
                                                v_f.reshape(b, t, main_w),
                                                sb_past[0], sb_past[1], sb_past[2], bias, pages=8)
            else:
                y_main = _stick_breaking_prompt(proj3, k_bf.reshape(b, t, main_w),
                                                v_bf.reshape(b, t, main_w), bias, n_heads, 256, 12)
        if decode:
            t_pad = 8
            proj_pad = jnp.pad(proj3, ((0, 0), (0, t_pad - t), (0, 0)))
            y_mem = _mem_attend(proj_pad, mem_k[l], mem_v[l], mem_w, t_pad)[:, :t]
        else:
            y_mem = _mem_attend(proj3, mem_k[l], mem_v[l], mem_w, min(512, t))
        x1, lt = _out_ln_route(y_main.reshape(n, main_w), y_mem.reshape(n, mem_w), x,
                               p["w_out"], l, p["ln1_g"][l], p["ln1_b"][l],
                               p["r_cat"][l], p["r_hi"][l], alpha, tm_c)
        x, xb = _hier_moe_ln(x1, lt, p["r_bias"][l], p["w_gate"], p["w_up"], p["w_down"], l,
                             p["ln2_g"][l], p["ln2_b"][l], alpha, tm_moe, tm)
        if l == n_a - 1:
            if decode:
                k_f, v_f = _matmul(xb, p["w_sb_kv"], 0, [(0, main_w, F32), (main_w, main_w, F32)],
                                   n, "proj_sb_kv")
                k_out = k_f.reshape(b, t, n_heads, HEAD_DIM)
                v_out = v_f.reshape(b, t, n_heads, HEAD_DIM)
            else:
                k_hm, v_hm, k_bf, v_bf = _kv_project(xb, p["w_sb_kv"][0], b, t, 256)
                k_out = jnp.transpose(k_hm, (0, 2, 1, 3))
                v_out = jnp.transpose(v_hm, (0, 2, 1, 3))
    return x.reshape(b, t, d), jnp.stack(new_pool, axis=0), k_out, v_out


def kernel(x_prompt, x_sample, state_pool, cache_sb_k, cache_sb_v, cache_mem_k, cache_mem_v,
           page_table, mem_prompt, w_in, w_out, w_pool_grp, pool_scale, w_mem_k, w_mem_v,
           ln1_g, ln1_b, ln2_g, ln2_b, w_route_grp, b_route_grp, w_route_exp, b_route_exp,
           w_gate, w_up, w_down, w_sb_k, w_sb_v, sb_bias):
    depth, d, _ = w_in.shape
    bp, _, _ = x_prompt.shape
    bs, ts, _ = x_sample.shape
    n_a = w_pool_grp.shape[0]
    main_w = w_sb_k.shape[1]
    mem_w = w_mem_k.shape[2]
    n_mem = mem_prompt.shape[1]
    n_heads = main_w // HEAD_DIM
    mem_heads = mem_w // HEAD_DIM

    wr = jnp.concatenate([w_route_grp, w_route_exp.reshape(depth, d, N_EXPERTS)], axis=-1)
    wr = jnp.pad(wr, ((0, 0), (0, 0), (0, LANES - N_GROUPS - N_EXPERTS)))
    r_hi = wr.astype(BF)
    r_lo = (wr - r_hi.astype(F32)).astype(BF)
    r_cat = jnp.concatenate([r_hi, r_lo], axis=-1)
    r_bias = jnp.pad(jnp.concatenate([b_route_grp, b_route_exp.reshape(depth, N_EXPERTS)], axis=-1),
                     ((0, 0), (0, ROUTE_ROWS - N_GROUPS - N_EXPERTS))).reshape(depth, ROUTE_ROWS, 1)
    p = dict(
        w_in=w_in.astype(BF), w_out=w_out.astype(BF), w_pool=w_pool_grp.astype(BF),
        pool_scale=pool_scale, w_sb_kv=jnp.concatenate([w_sb_k, w_sb_v], axis=1).astype(BF)[None],
        ln1_g=ln1_g, ln1_b=ln1_b, ln2_g=ln2_g, ln2_b=ln2_b, r_hi=r_hi, r_cat=r_cat, r_bias=r_bias,
        w_gate=w_gate, w_up=w_up, w_down=w_down, sb_bias=sb_bias)

    w_mem = jnp.concatenate([jnp.transpose(w_mem_k, (1, 0, 2)).reshape(d, depth * mem_w),
                             jnp.transpose(w_mem_v, (1, 0, 2)).reshape(d, depth * mem_w)],
                            axis=1).astype(BF)
    mem_kv = _mem_project(mem_prompt.reshape(bp * n_mem, d).astype(BF), w_mem, mem_w)
    mem_kv = mem_kv.reshape(2, depth, bp, n_mem, mem_w)
    pool0 = jnp.zeros((n_a, bp, POOL_STATE, main_w), x_prompt.dtype)
    y_p, pool_p, k_p, v_p = _trunk(x_prompt, pool0, mem_kv[0], mem_kv[1], None, p, decode=False)

    sb_past = (cache_sb_k, cache_sb_v, page_table)
    y_s, pool_s, k_s, v_s = _trunk(
        x_sample, state_pool, cache_mem_k.reshape(depth, bs, n_mem, mem_w),
        cache_mem_v.reshape(depth, bs, n_mem, mem_w), sb_past, p, decode=True)

    return (y_p, y_s, pool_p, pool_s, k_p, v_p, k_s, v_s,
            mem_kv[0].reshape(depth, bp, n_mem, mem_heads, HEAD_DIM),
            mem_kv[1].reshape(depth, bp, n_mem, mem_heads, HEAD_DIM))
```
